```python
import math
import jax
import jax.numpy as jnp
from jax import lax
import numpy as np

D_MODEL = 1024
BATCH = 8
SEQ = 4096
DEPTH = 4

CHUNK = 64
Q_BLOCK = 128
NORM_EPS = 1e-6

DA_HEADS = 4
DA_HEAD_DIM = 64
DA_V_DIM = 2 * DA_HEAD_DIM
DA_WIDTH = DA_HEADS * DA_V_DIM
ROT_DIM = DA_HEAD_DIM // 4
ROPE_THETA = 500000.0

SSM_GROUP = 16
SSM_GROUPS = 16
SSM_WIDTH = SSM_GROUP * SSM_GROUPS
SSM_STATE = 64

RW_HEADS = 4
RW_HEAD_DIM = 64
RW_WIDTH = RW_HEADS * RW_HEAD_DIM
RW_DECAY_RANK = 32
RW_AAA_RANK = 32
RW_GATE_RANK = 64
RW_LN_EPS = 64e-5

MIX_WIDTH = DA_WIDTH + SSM_WIDTH + RW_WIDTH
DA_COLS = 3 * DA_WIDTH
RW_COLS = 3 * RW_WIDTH + RW_DECAY_RANK + RW_AAA_RANK + RW_GATE_RANK
IN_COLS = DA_COLS + SSM_WIDTH + RW_COLS

PEER_HEADS = 8
N_KEYS = 128
N_EXPERTS = N_KEYS * N_KEYS
PEER_KEY_DIM = 128
PEER_TOPK = 16
PEER_TOKEN_BLOCK = 128

kernel_name = 'hybrid_diffattn_s5_rwkv7_peer'


def rms_norm(x, g, eps=NORM_EPS):
    xf = x.astype(jnp.float32)
    y = xf * lax.rsqrt(jnp.mean(xf * xf, axis=-1, keepdims=True) + eps)
    return (y * g.astype(jnp.float32)).astype(x.dtype)


def partial_rope(t, cos, sin):
    half = ROT_DIM // 2
    c = cos[:, :, None, None, :].astype(t.dtype)
    s = sin[:, :, None, None, :].astype(t.dtype)
    r1, r2, rest = t[..., :half], t[..., half:ROT_DIM], t[..., ROT_DIM:]
    return jnp.concatenate([r1 * c - r2 * s, r2 * c + r1 * s, rest], axis=-1)


def diff_attention(cols, cos, sin, q_norm, k_norm, lam, subln, layer):
    bsz, seq, _ = cols.shape
    dt = cols.dtype
    q = cols[..., :DA_WIDTH].reshape(bsz, seq, DA_HEADS, 2, DA_HEAD_DIM)
    k = cols[..., DA_WIDTH:2 * DA_WIDTH].reshape(bsz, seq, DA_HEADS, 2, DA_HEAD_DIM)
    v = cols[..., 2 * DA_WIDTH:].reshape(bsz, seq, DA_HEADS, DA_V_DIM)
    q = partial_rope(rms_norm(q, q_norm), cos, sin)
    k = partial_rope(rms_norm(k, k_norm), cos, sin)
    lam_init = 0.8 - 0.6 * math.exp(-0.3 * layer)
    lf = lam.astype(jnp.float32)
    lam_full = jnp.exp(jnp.sum(lf[0] * lf[1])) - jnp.exp(jnp.sum(lf[2] * lf[3])) + lam_init
    scale = DA_HEAD_DIM ** -0.5
    outs = []
    for blk in range(seq // Q_BLOCK):
        q0 = blk * Q_BLOCK
        kv_end = q0 + Q_BLOCK
        s = jnp.einsum('bqhmd,bkhmd->bhmqk', q[:, q0:kv_end], k[:, :kv_end]).astype(jnp.float32) * scale
        q_chunk = (q0 + jnp.arange(Q_BLOCK)) // CHUNK
        k_chunk = jnp.arange(kv_end) // CHUNK
        s = jnp.where(k_chunk[None, :] <= q_chunk[:, None], s, -jnp.inf)
        p = jax.nn.softmax(s, axis=-1)
        a = p[:, :, 0] - lam_full * p[:, :, 1]
        outs.append(jnp.einsum('bhqk,bkhe->bqhe', a.astype(dt), v[:, :kv_end]))
    o = jnp.concatenate(outs, axis=1)
    o = rms_norm(o, subln) * (1.0 - lam_init)
    return o.reshape(bsz, seq, DA_WIDTH)


def _complex_affine_combine(e1, e2):
    a1r, a1i, b1r, b1i = e1
    a2r, a2i, b2r, b2i = e2
    ar = a2r * a1r - a2i * a1i
    ai = a2r * a1i + a2i * a1r
    br = a2r * b1r - a2i * b1i + b2r
    bi = a2r * b1i + a2i * b1r + b2i
    return (ar, ai, br, bi)


def s5_mixer(u, lam_re, lam_im, b_re, b_im, c_re, c_im, d_skip, log_dt, w_glu, b_glu, out_norm):
    bsz, seq, _ = u.shape
    f32 = jnp.float32
    uf = u.astype(f32).reshape(bsz, seq, SSM_GROUPS, SSM_GROUP)
    lr = jnp.minimum(lam_re.astype(f32), -1e-4)
    li = lam_im.astype(f32)
    step = jnp.exp(log_dt.astype(f32))[:, None]
    mag = jnp.exp(lr * step)
    abar_re = mag * jnp.cos(li * step)
    abar_im = mag * jnp.sin(li * step)
    den = lr * lr + li * li
    coef_re = ((abar_re - 1.0) * lr + abar_im * li) / den
    coef_im = (abar_im * lr - (abar_re - 1.0) * li) / den
    br, bi = b_re.astype(f32), b_im.astype(f32)
    bbar_re = coef_re[..., None] * br - coef_im[..., None] * bi
    bbar_im = coef_re[..., None] * bi + coef_im[..., None] * br
    bu_re = jnp.einsum('blgh,gph->lbgp', uf, bbar_re)
    bu_im = jnp.einsum('blgh,gph->lbgp', uf, bbar_im)
    a_re = jnp.broadcast_to(abar_re[None, None], (seq, 1, SSM_GROUPS, SSM_STATE))
    a_im = jnp.broadcast_to(abar_im[None, None], (seq, 1, SSM_GROUPS, SSM_STATE))
    _, _, xr, xi = lax.associative_scan(_complex_affine_combine, (a_re, a_im, bu_re, bu_im), axis=0)
    y = (jnp.einsum('lbgp,ghp->blgh', xr, c_re.astype(f32))
         - jnp.einsum('lbgp,ghp->blgh', xi, c_im.astype(f32))
         + d_skip.astype(f32) * uf)
    yg = jax.nn.gelu(y.reshape(bsz, seq, SSM_WIDTH), approximate=False)
    out = yg * jax.nn.sigmoid(yg @ w_glu.astype(f32) + b_glu.astype(f32))
    return rms_norm(out, out_norm).astype(u.dtype)


def rwkv7_mixer(cols, mu, decay_up, decay_w0, aaa_up, aaa_a0, gate_up, k_k, k_a, r_k, ln_w, ln_b):
    bsz, seq, _ = cols.shape
    f32 = jnp.float32
    c = cols.astype(f32)
    prev = jnp.pad(c, ((0, 0), (1, 0), (0, 0)))[:, :-1]
    c = c + (prev - c) * mu.astype(f32)
    o1, o2, o3 = RW_WIDTH, 2 * RW_WIDTH, 3 * RW_WIDTH
    o4, o5 = o3 + RW_DECAY_RANK, o3 + RW_DECAY_RANK + RW_AAA_RANK
    r, k, v = c[..., :o1], c[..., o1:o2], c[..., o2:o3]
    wd, ad, gd = c[..., o3:o4], c[..., o4:o5], c[..., o5:]
    w_log = -jax.nn.softplus(-(decay_w0.astype(f32) + jnp.tanh(wd) @ decay_up.astype(f32))) - 0.5
    decay = jnp.exp(-jnp.exp(w_log))
    a = jax.nn.sigmoid(aaa_a0.astype(f32) + ad @ aaa_up.astype(f32))
    g = jax.nn.sigmoid(gd) @ gate_up.astype(f32)
    heads = lambda t: t.reshape(bsz, seq, RW_HEADS, RW_HEAD_DIM)
    kk = heads(k * k_k.astype(f32))
    kk = kk / jnp.maximum(jnp.sqrt(jnp.sum(kk * kk, axis=-1, keepdims=True)), 1e-12)
    k = k * (1.0 + (a - 1.0) * k_a.astype(f32))
    r, k, v, decay, a = heads(r), heads(k), heads(v), heads(decay), heads(a)
    bvec = kk * a

    def step(S, inp):
        r_t, w_t, k_t, v_t, kk_t, b_t = inp
        sa = jnp.einsum('bhij,bhj->bhi', S, -kk_t)
        S = S * w_t[:, :, None, :] + sa[..., None] * b_t[:, :, None, :] + v_t[..., None] * k_t[:, :, None, :]
        return S, jnp.einsum('bhij,bhj->bhi', S, r_t)

    xs = tuple(jnp.moveaxis(t, 1, 0) for t in (r, decay, k, v, kk, bvec))
    S0 = jnp.zeros((bsz, RW_HEADS, RW_HEAD_DIM, RW_HEAD_DIM), f32)
    _, y = lax.scan(step, S0, xs)
    y = jnp.moveaxis(y, 0, 1)
    mean = jnp.mean(y, axis=-1, keepdims=True)
    var = jnp.mean(jnp.square(y - mean), axis=-1, keepdims=True)
    y = ((y - mean) * lax.rsqrt(var + RW_LN_EPS)).reshape(bsz, seq, RW_WIDTH)
    y = y * ln_w.astype(f32) + ln_b.astype(f32)
    bonus = jnp.sum(r * k * r_k.astype(f32), axis=-1, keepdims=True) * v
    y = y + bonus.reshape(bsz, seq, RW_WIDTH)
    return (y * g).astype(cols.dtype)


def peer_ffn(h, w_q, sub_keys, u_tab, v_tab):
    bsz, seq, dm = h.shape
    blocks = h.reshape(bsz * seq // PEER_TOKEN_BLOCK, PEER_TOKEN_BLOCK, dm)

    def block_fn(hb):
        q = (hb @ w_q).reshape(PEER_TOKEN_BLOCK, PEER_HEADS, 2, PEER_KEY_DIM)
        s = jnp.einsum('thmd,hmkd->thmk', q, sub_keys).astype(jnp.float32)
        top_s, top_i = lax.top_k(s, PEER_TOPK)
        cand_s = top_s[:, :, 0, :, None] + top_s[:, :, 1, None, :]
        cand_i = top_i[:, :, 0, :, None] * N_KEYS + top_i[:, :, 1, None, :]
        cand_s = cand_s.reshape(PEER_TOKEN_BLOCK, PEER_HEADS, PEER_TOPK * PEER_TOPK)
        cand_i = cand_i.reshape(PEER_TOKEN_BLOCK, PEER_HEADS, PEER_TOPK * PEER_TOPK)
        best_s, best_pos = lax.top_k(cand_s, PEER_TOPK)
        expert = jnp.take_along_axis(cand_i, best_pos, axis=-1)
        gate = jax.nn.softmax(best_s, axis=-1)
        u_sel = jnp.take(u_tab, expert, axis=0)
        act = jax.nn.gelu(jnp.einsum('thkd,td->thk', u_sel, hb).astype(jnp.float32), approximate=False)
        v_sel = jnp.take(v_tab, expert, axis=0)
        return jnp.einsum('thk,thkd->td', (gate * act).astype(hb.dtype), v_sel)

    return lax.map(block_fn, blocks).reshape(bsz, seq, dm)


def setup_inputs(seed: int = 0) -> dict:
    key = jax.random.key(seed)
    keys = list(jax.random.split(key, 48))
    f32 = jnp.float32

    def nxt():
        return keys.pop()

    def nrm(shape, scale):
        return jax.random.normal(nxt(), shape, f32) * scale

    def gain(shape):
        return 1.0 + nrm(shape, 0.02)

    x = nrm((BATCH, SEQ, D_MODEL), 1.0)
    offs = jax.random.randint(nxt(), (BATCH, 1), 0, 64, dtype=jnp.int32) * CHUNK
    positions = (offs + jnp.arange(SEQ, dtype=jnp.int32)[None, :]).astype(jnp.int32)
    n_idx = jnp.arange(SSM_STATE, dtype=f32)
    decay_base = jnp.tile(jnp.linspace(-6.0, -1.0, RW_HEAD_DIM, dtype=f32), RW_HEADS)
    return {
        'x': x,
        'positions': positions,
        'norm_mix': gain((DEPTH, D_MODEL)),
        'w_in': nrm((DEPTH, D_MODEL, IN_COLS), D_MODEL ** -0.5),
        'da_q_norm': gain((DEPTH, DA_HEADS, 2, DA_HEAD_DIM)),
        'da_k_norm': gain((DEPTH, DA_HEADS, 2, DA_HEAD_DIM)),
        'da_lambda': nrm((DEPTH, 4, DA_HEAD_DIM), 0.1),
        'da_subln': gain((DEPTH, DA_V_DIM)),
        'ssm_lambda_re': -0.5 + nrm((DEPTH, SSM_GROUPS, SSM_STATE), 0.01),
        'ssm_lambda_im': math.pi * n_idx + nrm((DEPTH, SSM_GROUPS, SSM_STATE), 0.01),
        'ssm_b_re': nrm((DEPTH, SSM_GROUPS, SSM_STATE, SSM_GROUP), (2 * SSM_GROUP) ** -0.5),
        'ssm_b_im': nrm((DEPTH, SSM_GROUPS, SSM_STATE, SSM_GROUP), (2 * SSM_GROUP) ** -0.5),
        'ssm_c_re': nrm((DEPTH, SSM_GROUPS, SSM_GROUP, SSM_STATE), (2 * SSM_STATE) ** -0.5),
        'ssm_c_im': nrm((DEPTH, SSM_GROUPS, SSM_GROUP, SSM_STATE), (2 * SSM_STATE) ** -0.5),
        'ssm_d': nrm((DEPTH, SSM_GROUPS, SSM_GROUP), 1.0),
        'ssm_log_dt': jax.random.uniform(nxt(), (DEPTH, SSM_GROUPS), f32, math.log(1e-3), math.log(1e-1)),
        'ssm_w_glu': nrm((DEPTH, SSM_WIDTH, SSM_WIDTH), SSM_WIDTH ** -0.5),
        'ssm_b_glu': nrm((DEPTH, SSM_WIDTH), 0.02),
        'ssm_out_norm': gain((DEPTH, SSM_WIDTH)),
        'rw_mu': jax.random.uniform(nxt(), (DEPTH, RW_COLS), f32, 0.0, 1.0),
        'rw_decay_up': nrm((DEPTH, RW_DECAY_RANK, RW_WIDTH), 0.1),
        'rw_decay_w0': decay_base + nrm((DEPTH, RW_WIDTH), 0.1),
        'rw_aaa_up': nrm((DEPTH, RW_AAA_RANK, RW_WIDTH), 0.1),
        'rw_aaa_a0': nrm((DEPTH, RW_WIDTH), 0.1),
        'rw_gate_up': nrm((DEPTH, RW_GATE_RANK, RW_WIDTH), RW_GATE_RANK ** -0.5),
        'rw_k_k': 0.85 + nrm((DEPTH, RW_WIDTH), 0.02),
        'rw_k_a': gain((DEPTH, RW_WIDTH)),
        'rw_r_k': nrm((DEPTH, RW_HEADS, RW_HEAD_DIM), 0.1),
        'rw_ln_w': gain((DEPTH, RW_WIDTH)),
        'rw_ln_b': nrm((DEPTH, RW_WIDTH), 0.02),
        'w_out': nrm((DEPTH, MIX_WIDTH, D_MODEL), MIX_WIDTH ** -0.5),
        'norm_ffn': gain((DEPTH, D_MODEL)),
        'peer_w_q': nrm((DEPTH, D_MODEL, PEER_HEADS * 2 * PEER_KEY_DIM), D_MODEL ** -0.5),
        'peer_sub_keys': nrm((DEPTH, PEER_HEADS, 2, N_KEYS, PEER_KEY_DIM), PEER_KEY_DIM ** -0.5),
        'peer_u': nrm((DEPTH, N_EXPERTS, D_MODEL), D_MODEL ** -0.5),
        'peer_v': nrm((DEPTH, N_EXPERTS, D_MODEL), PEER_TOPK ** -0.5),
    }


def reference(x, positions, norm_mix, w_in, da_q_norm, da_k_norm, da_lambda, da_subln,
              ssm_lambda_re, ssm_lambda_im, ssm_b_re, ssm_b_im, ssm_c_re, ssm_c_im, ssm_d,
              ssm_log_dt, ssm_w_glu, ssm_b_glu, ssm_out_norm, rw_mu, rw_decay_up, rw_decay_w0,
              rw_aaa_up, rw_aaa_a0, rw_gate_up, rw_k_k, rw_k_a, rw_r_k, rw_ln_w, rw_ln_b,
              w_out, norm_ffn, peer_w_q, peer_sub_keys, peer_u, peer_v):
    inv_freq = ROPE_THETA ** (-jnp.arange(0, ROT_DIM, 2, dtype=jnp.float32) / ROT_DIM)
    ang = positions.astype(jnp.float32)[..., None] * inv_freq
    cos, sin = jnp.cos(ang), jnp.sin(ang)
    for l in range(DEPTH):
        h = rms_norm(x, norm_mix[l])
        cols = h @ w_in[l]
        cols_a = cols[..., :DA_COLS]
        cols_b = cols[..., DA_COLS:DA_COLS + SSM_WIDTH]
        cols_c = cols[..., DA_COLS + SSM_WIDTH:]
        y_a = diff_attention(cols_a, cos, sin, da_q_norm[l], da_k_norm[l], da_lambda[l], da_subln[l], l)
        y_b = s5_mixer(cols_b, ssm_lambda_re[l], ssm_lambda_im[l], ssm_b_re[l], ssm_b_im[l],
                       ssm_c_re[l], ssm_c_im[l], ssm_d[l], ssm_log_dt[l], ssm_w_glu[l],
                       ssm_b_glu[l], ssm_out_norm[l])
        y_c = rwkv7_mixer(cols_c, rw_mu[l], rw_decay_up[l], rw_decay_w0[l], rw_aaa_up[l],
                          rw_aaa_a0[l], rw_gate_up[l], rw_k_k[l], rw_k_a[l], rw_r_k[l],
                          rw_ln_w[l], rw_ln_b[l])
        mixed = jnp.concatenate([y_a.astype(x.dtype), y_b.astype(x.dtype), y_c.astype(x.dtype)], axis=-1)
        x = x + mixed @ w_out[l]
        x = x + peer_ffn(rms_norm(x, norm_ffn[l]), peer_w_q[l], peer_sub_keys[l], peer_u[l], peer_v[l])
    return x
```

```python
import functools
import math

import jax
import jax.numpy as jnp
from jax import lax
from jax.experimental import pallas as pl
from jax.experimental.pallas import tpu as pltpu

D_MODEL = 1024
CHUNK = 64
NORM_EPS = 1e-6

DA_HEADS = 4
DA_HEAD_DIM = 64
DA_V_DIM = 128
DA_WIDTH = 512
ROT_DIM = 16
ROPE_THETA = 500000.0

SSM_GROUP = 16
SSM_GROUPS = 16
SSM_WIDTH = 256
SSM_STATE = 64

RW_HEADS = 4
RW_HEAD_DIM = 64
RW_WIDTH = 256
RW_DECAY_RANK = 32
RW_AAA_RANK = 32
RW_GATE_RANK = 64
RW_LN_EPS = 64e-5
RW_COLS = 896

DA_COLS = 1536

PEER_HEADS = 8
N_KEYS = 128
PEER_KEY_DIM = 128
PEER_TOPK = 16
PEER_SLOTS = PEER_HEADS * PEER_TOPK
PEER_ROW_WORDS = D_MODEL // 2
PEER_ROW_SUBLANES = PEER_ROW_WORDS // 128

LANES = 128
SUBLANES = 8
VMEM_LIMIT_BYTES = 56 * 1024 * 1024

F32 = jnp.float32
BF16 = jnp.bfloat16


def _cparams(n_axes, vmem=VMEM_LIMIT_BYTES):
    return pltpu.CompilerParams(
        dimension_semantics=("arbitrary",) * n_axes, vmem_limit_bytes=vmem)


def _dot(a, b):
    return jnp.dot(a, b, preferred_element_type=F32)


def _split(x):
    hi = x.astype(BF16)
    lo = (x - hi.astype(F32)).astype(BF16)
    return hi, lo


def _dot_x2(x, w):
    hi, lo = _split(x)
    return _dot(hi, w) + _dot(lo, w)


def _dot_x3(x, w_hi, w_lo):
    hi, lo = _split(x)
    return _dot(hi, w_hi) + _dot(lo, w_hi) + _dot(hi, w_lo)


def _wsplit(w):
    hi = w.astype(BF16)
    lo = (w - hi.astype(F32)).astype(BF16)
    return hi, lo


def _sigmoid(x):
    return 1.0 / (1.0 + jnp.exp(-x))


def _gelu(x):
    return 0.5 * x * (1.0 + lax.erf(x * (1.0 / math.sqrt(2.0))))


def _seg_ones(n, seg):
    r = jnp.arange(n) // seg
    return (r[:, None] == r[None, :]).astype(BF16)


def _full(shape):
    nd = len(shape)
    return pl.BlockSpec(shape, lambda *_: (0,) * nd)


def _inproj_body(x_ref, g_ref, wa_ref, wb_ref, wc_ref, a_ref, b_ref, c_ref):
    x = x_ref[...]
    ms = jnp.mean(x * x, axis=-1, keepdims=True)
    h = (x * lax.rsqrt(ms + NORM_EPS) * g_ref[...]).astype(BF16)
    a_ref[...] = _dot(h, wa_ref[...])
    b_ref[...] = _dot(h, wb_ref[...])
    c_ref[...] = _dot(h, wc_ref[...])


def _inproj(x, g, wa, wb, wc, tl):
    bsz, seq, _ = x.shape
    return pl.pallas_call(
        _inproj_body,
        grid=(bsz, seq // tl),
        in_specs=[
            pl.BlockSpec((None, tl, D_MODEL), lambda b, i: (b, i, 0)),
            _full((1, D_MODEL)),
            _full((D_MODEL, DA_COLS)),
            _full((D_MODEL, SSM_WIDTH)),
            _full((D_MODEL, RW_COLS)),
        ],
        out_specs=[
            pl.BlockSpec((None, tl, DA_COLS), lambda b, i: (b, i, 0)),
            pl.BlockSpec((tl, SSM_WIDTH), lambda b, i: (i, b)),
            pl.BlockSpec((tl, RW_COLS), lambda b, i: (i, b)),
        ],
        out_shape=[
            jax.ShapeDtypeStruct((bsz, seq, DA_COLS), F32),
            jax.ShapeDtypeStruct((seq, bsz * SSM_WIDTH), F32),
            jax.ShapeDtypeStruct((seq, bsz * RW_COLS), F32),
        ],
        compiler_params=_cparams(2),
        name="inproj",
    )(x, g, wa, wb, wc)


def _rope_body(pos_ref, invf_ref, c_ref, s1_ref, s2_ref):
    ang = pos_ref[...].astype(F32) * invf_ref[...]
    c, s = jnp.cos(ang), jnp.sin(ang)
    lm = lax.broadcasted_iota(jnp.int32, ang.shape, 1) % DA_HEAD_DIM
    half = ROT_DIM // 2
    c_ref[...] = jnp.where(lm < ROT_DIM, c, 1.0)
    s1_ref[...] = jnp.where(lm < half, -s, 0.0)
    s2_ref[...] = jnp.where((lm >= half) & (lm < ROT_DIM), s, 0.0)


def _rope_tables(positions, tm):
    t = positions.size
    half = ROT_DIM // 2
    inv_freq = ROPE_THETA ** (-jnp.arange(0, ROT_DIM, 2, dtype=F32) / ROT_DIM)
    invf = inv_freq[jnp.arange(LANES) % half][None, :]
    pos = jnp.broadcast_to(positions.reshape(t, 1), (t, LANES))
    spec = pl.BlockSpec((tm, LANES), lambda i: (i, 0))
    return pl.pallas_call(
        _rope_body,
        grid=(t // tm,),
        in_specs=[spec, _full((1, LANES))],
        out_specs=[spec, spec, spec],
        out_shape=[jax.ShapeDtypeStruct((t, LANES), F32)] * 3,
        compiler_params=_cparams(1),
        name="rope_tables",
    )(pos, invf)


def _attn_prep_body(a_ref, c_ref, s1_ref, s2_ref, qg_ref, kg_ref, seg_ref,
                    q_ref, k_ref, v_ref):
    reps = DA_WIDTH // LANES
    c = jnp.tile(c_ref[...], (1, reps))
    s1 = jnp.tile(s1_ref[...], (1, reps))
    s2 = jnp.tile(s2_ref[...], (1, reps))
    seg = seg_ref[...]
    half = ROT_DIM // 2

    def norm_rope(t, gain):
        ms = _dot_x2(t * t, seg) * (1.0 / DA_HEAD_DIM)
        tn = t * lax.rsqrt(ms + NORM_EPS) * gain
        up = pltpu.roll(tn, DA_WIDTH - half, axis=1)
        dn = pltpu.roll(tn, half, axis=1)
        return tn * c + up * s1 + dn * s2

    q = norm_rope(a_ref[:, 0:DA_WIDTH], qg_ref[...])
    k = norm_rope(a_ref[:, DA_WIDTH:2 * DA_WIDTH], kg_ref[...])
    q_ref[...] = (q * (DA_HEAD_DIM ** -0.5)).astype(BF16)
    k_ref[...] = k.astype(BF16)
    v_ref[...] = a_ref[:, 2 * DA_WIDTH:3 * DA_WIDTH].astype(BF16)


def _attn_prep(cols_a, rope, qg, kg, tm):
    t = cols_a.shape[0]
    c, s1, s2 = rope
    lane_spec = pl.BlockSpec((tm, LANES), lambda i: (i, 0))
    out_spec = pl.BlockSpec((tm, DA_WIDTH), lambda i: (i, 0))
    return pl.pallas_call(
        _attn_prep_body,
        grid=(t // tm,),
        in_specs=[
            pl.BlockSpec((tm, DA_COLS), lambda i: (i, 0)),
            lane_spec, lane_spec, lane_spec,
            _full((1, DA_WIDTH)), _full((1, DA_WIDTH)),
            _full((DA_WIDTH, DA_WIDTH)),
        ],
        out_specs=[out_spec, out_spec, out_spec],
        out_shape=[jax.ShapeDtypeStruct((t, DA_WIDTH), BF16)] * 3,
        compiler_params=_cparams(1),
        name="attn_prep",
    )(cols_a, c, s1, s2, qg, kg, _seg_ones(DA_WIDTH, DA_HEAD_DIM))


def _attn_body(q_ref, k_ref, v_ref, lam_ref, sub_ref, o_ref, m_ref, l_ref, acc_ref, *, tq):
    qi = pl.program_id(2)
    q = q_ref[...]
    lane = lax.broadcasted_iota(jnp.int32, q.shape, 1)
    zero = jnp.zeros_like(q)
    qmaps = (jnp.where(lane < DA_HEAD_DIM, q, zero), jnp.where(lane >= DA_HEAD_DIM, q, zero))

    m_ref[...] = jnp.full(m_ref.shape, -jnp.inf, F32)
    l_ref[...] = jnp.zeros(l_ref.shape, F32)
    acc_ref[...] = jnp.zeros(acc_ref.shape, F32)

    row_chunk = lax.broadcasted_iota(jnp.int32, (tq, tq), 0) // CHUNK
    col_chunk = lax.broadcasted_iota(jnp.int32, (tq, tq), 1) // CHUNK
    causal = col_chunk <= row_chunk

    def tile(kt, masked):
        start = pl.multiple_of(kt * tq, tq)
        k = k_ref[pl.ds(start, tq), :]
        v = v_ref[pl.ds(start, tq), :]
        for m in range(2):
            s = lax.dot_general(qmaps[m], k, (((1,), (1,)), ((), ())),
                                preferred_element_type=F32)
            if masked:
                s = jnp.where(causal, s, -jnp.inf)
            m_old = m_ref[m]
            m_new = jnp.maximum(m_old, jnp.max(s, axis=1, keepdims=True))
            alpha = jnp.exp(m_old - m_new)
            p = jnp.exp(s - jnp.tile(m_new, (1, tq // LANES)))
            l_ref[m] = alpha * l_ref[m] + jnp.sum(p, axis=1, keepdims=True)
            acc_ref[m] = alpha * acc_ref[m] + _dot(p.astype(BF16), v)
            m_ref[m] = m_new

    def full_tile(kt, carry):
        tile(kt, False)
        return carry

    lax.fori_loop(0, qi, full_tile, 0)
    tile(qi, True)

    lam = lam_ref[...]
    lam_full = (jnp.exp(jnp.sum(lam[0:1] * lam[1:2], axis=1, keepdims=True))
                - jnp.exp(jnp.sum(lam[2:3] * lam[3:4], axis=1, keepdims=True))
                + lam[4:5, 0:1])
    lam_init = lam[4:5, 0:1]
    o = acc_ref[0] / l_ref[0] - lam_full * (acc_ref[1] / l_ref[1])
    ms = jnp.mean(o * o, axis=-1, keepdims=True)
    o_ref[...] = o * lax.rsqrt(ms + NORM_EPS) * sub_ref[...] * (1.0 - lam_init)


def _attention(q, k, v, lam_pad, subln, tq):
    bsz, seq, _ = q.shape
    kv_spec = pl.BlockSpec((None, seq, DA_V_DIM), lambda b, h, i: (b, 0, h))
    return pl.pallas_call(
        functools.partial(_attn_body, tq=tq),
        grid=(bsz, DA_HEADS, seq // tq),
        in_specs=[
            pl.BlockSpec((None, tq, DA_V_DIM), lambda b, h, i: (b, i, h)),
            kv_spec, kv_spec,
            _full((SUBLANES, LANES)),
            _full((1, DA_V_DIM)),
        ],
        out_specs=pl.BlockSpec((None, tq, DA_V_DIM), lambda b, h, i: (b, i, h)),
        out_shape=jax.ShapeDtypeStruct((bsz, seq, DA_WIDTH), F32),
        scratch_shapes=[
            pltpu.VMEM((2, tq, LANES), F32),
            pltpu.VMEM((2, tq, LANES), F32),
            pltpu.VMEM((2, tq, DA_V_DIM), F32),
        ],
        compiler_params=_cparams(3),
        name="diff_attention",
    )(q, k, v, lam_pad, subln)


def _s5_body(u_ref, bhi_ref, blo_ref, are_ref, aim_ref, chi_ref, clo_ref, d_ref,
             whi_ref, wlo_ref, bg_ref, g_ref, o_ref, st_ref, x_ref, *, tt, bsz):
    ns = SSM_GROUPS * SSM_STATE

    @pl.when(pl.program_id(0) == 0)
    def _():
        st_ref[...] = jnp.zeros(st_ref.shape, F32)

    u = u_ref[...]
    x_ref[...] = _dot_x3(u, bhi_ref[...], blo_ref[...])
    a_re = jnp.broadcast_to(are_ref[...], (bsz, ns))
    a_im = jnp.broadcast_to(aim_ref[...], (bsz, ns))

    def step(t, carry):
        xr, xi = carry
        r = pl.multiple_of(t * bsz, bsz)
        nr = a_re * xr - a_im * xi + x_ref[pl.ds(r, bsz), 0:ns]
        ni = a_re * xi + a_im * xr + x_ref[pl.ds(r, bsz), ns:2 * ns]
        x_ref[pl.ds(r, bsz), 0:ns] = nr
        x_ref[pl.ds(r, bsz), ns:2 * ns] = ni
        return nr, ni

    xr, xi = lax.fori_loop(0, tt, step, (st_ref[:, 0:ns], st_ref[:, ns:2 * ns]))
    st_ref[:, 0:ns] = xr
    st_ref[:, ns:2 * ns] = xi

    y = _dot_x3(x_ref[...], chi_ref[...], clo_ref[...]) + d_ref[...] * u
    yg = _gelu(y)
    z = _dot_x3(yg, whi_ref[...], wlo_ref[...]) + bg_ref[...]
    out = yg * _sigmoid(z)
    ms = jnp.mean(out * out, axis=-1, keepdims=True)
    o_ref[...] = out * lax.rsqrt(ms + NORM_EPS) * g_ref[...]


def _s5(u, ops, tt, bsz):
    rows = u.shape[0]
    ns2 = 2 * SSM_GROUPS * SSM_STATE
    blk = tt * bsz
    io_spec = pl.BlockSpec((blk, SSM_WIDTH), lambda i: (i, 0))
    return pl.pallas_call(
        functools.partial(_s5_body, tt=tt, bsz=bsz),
        grid=(rows // blk,),
        in_specs=[
            io_spec,
            _full((SSM_WIDTH, ns2)), _full((SSM_WIDTH, ns2)),
            _full((1, ns2 // 2)), _full((1, ns2 // 2)),
            _full((ns2, SSM_WIDTH)), _full((ns2, SSM_WIDTH)),
            _full((1, SSM_WIDTH)),
            _full((SSM_WIDTH, SSM_WIDTH)), _full((SSM_WIDTH, SSM_WIDTH)),
            _full((1, SSM_WIDTH)), _full((1, SSM_WIDTH)),
        ],
        out_specs=io_spec,
        out_shape=jax.ShapeDtypeStruct((rows, SSM_WIDTH), F32),
        scratch_shapes=[
            pltpu.VMEM((bsz, ns2), F32),
            pltpu.VMEM((blk, ns2), F32),
        ],
        compiler_params=_cparams(1),
        name="s5",
    )(u, *ops)


def _s5_operands(lam_re, lam_im, b_re, b_im, c_re, c_im, d_skip, log_dt, w_glu, b_glu, out_norm):
    lr = jnp.minimum(lam_re, -1e-4)
    li = lam_im
    step = jnp.exp(log_dt)[:, None]
    mag = jnp.exp(lr * step)
    abar_re = mag * jnp.cos(li * step)
    abar_im = mag * jnp.sin(li * step)
    den = lr * lr + li * li
    coef_re = ((abar_re - 1.0) * lr + abar_im * li) / den
    coef_im = (abar_im * lr - (abar_re - 1.0) * li) / den
    bbar_re = coef_re[..., None] * b_re - coef_im[..., None] * b_im
    bbar_im = coef_re[..., None] * b_im + coef_im[..., None] * b_re
    eye = jnp.eye(SSM_GROUPS, dtype=F32)
    ns = SSM_GROUPS * SSM_STATE

    def in_mat(bb):
        return jnp.einsum('gph,gk->ghkp', bb, eye).reshape(SSM_WIDTH, ns)

    def out_mat(cc):
        return jnp.einsum('ghp,gk->kpgh', cc, eye).reshape(ns, SSM_WIDTH)

    bmat = jnp.concatenate([in_mat(bbar_re), in_mat(bbar_im)], axis=1)
    cmat = jnp.concatenate([out_mat(c_re), -out_mat(c_im)], axis=0)
    return (*_wsplit(bmat), abar_re.reshape(1, ns), abar_im.reshape(1, ns), *_wsplit(cmat),
            d_skip.reshape(1, SSM_WIDTH), *_wsplit(w_glu), b_glu.reshape(1, SSM_WIDTH),
            out_norm.reshape(1, SSM_WIDTH))


def _rwkv_body(c_ref, mu_ref, wdh_ref, wdl_ref, wah_ref, wal_ref, wgh_ref, wgl_ref,
               w0_ref, a0_ref, kk_ref, ka_ref, rk_ref, lnw_ref, lnb_ref,
               seg256_ref, seg128_ref, eye_ref,
               o_ref, prev_ref, st_ref, r_s, w_s, k_s, v_s, a_s, b_s, y_s, *, tt, bsz):
    n_pairs = RW_WIDTH // LANES
    n_inst = n_pairs * bsz
    hd = RW_HEAD_DIM

    @pl.when(pl.program_id(0) == 0)
    def _():
        prev_ref[...] = jnp.zeros(prev_ref.shape, F32)
        st_ref[...] = jnp.zeros(st_ref.shape, F32)

    c = c_ref[...]
    rows = c.shape[0]
    prev = jnp.concatenate([prev_ref[...], c[0:rows - bsz]], axis=0)
    prev_ref[...] = c[rows - bsz:rows]
    cs = c + (prev - c) * mu_ref[...]
    r = cs[:, 0:RW_WIDTH]
    k = cs[:, RW_WIDTH:2 * RW_WIDTH]
    v = cs[:, 2 * RW_WIDTH:3 * RW_WIDTH]
    x3 = cs[:, 3 * RW_WIDTH:RW_COLS]

    seg256 = seg256_ref[...]
    wdec = w0_ref[...] + _dot_x3(jnp.tanh(x3), wdh_ref[...], wdl_ref[...])
    z = -wdec
    softplus = jnp.maximum(z, 0.0) + jnp.log(1.0 + jnp.exp(-jnp.abs(z)))
    decay = jnp.exp(-jnp.exp(-softplus - 0.5))
    a = _sigmoid(a0_ref[...] + _dot_x3(x3, wah_ref[...], wal_ref[...]))
    g = _dot_x3(_sigmoid(x3), wgh_ref[...], wgl_ref[...])
    kk = k * kk_ref[...]
    kk = kk / jnp.maximum(jnp.sqrt(_dot_x2(kk * kk, seg256)), 1e-12)
    k2 = k * (1.0 + (a - 1.0) * ka_ref[...])
    r_s[...] = r
    w_s[...] = decay
    k_s[...] = k2
    v_s[...] = v
    a_s[...] = -kk
    b_s[...] = kk * a

    seg128 = seg128_ref[...]
    eye = eye_ref[...]
    sub = lax.broadcasted_iota(jnp.int32, (bsz, LANES), 0)

    def step(t, carry):
        row0 = pl.multiple_of(t * bsz, bsz)
        blk = [ref[pl.ds(row0, bsz), :] for ref in (r_s, w_s, k_s, v_s, a_s, b_s)]

        def vec(q, inst):
            p, b = divmod(inst, bsz)
            return blk[q][b:b + 1, p * LANES:(p + 1) * LANES]

        states = [st_ref[i * hd:(i + 1) * hd, :] for i in range(n_inst)]
        lhs_a = jnp.concatenate([states[i] * vec(4, i) for i in range(n_inst)], axis=0)
        lhs_v = jnp.concatenate([eye * vec(3, i) for i in range(n_inst)], axis=0)
        sa = _dot_x2(lhs_a, seg128)
        vcol = _dot_x2(lhs_v, seg128)
        new = []
        for i in range(n_inst):
            sl = slice(i * hd, (i + 1) * hd)
            new.append(states[i] * vec(1, i) + sa[sl] * vec(5, i) + vcol[sl] * vec(2, i))
        ycol = _dot_x2(jnp.concatenate([new[i] * vec(0, i) for i in range(n_inst)], axis=0), seg128)
        for i in range(n_inst):
            st_ref[i * hd:(i + 1) * hd, :] = new[i]
        for p in range(n_pairs):
            yblk = jnp.zeros((bsz, LANES), F32)
            for b in range(bsz):
                i = p * bsz + b
                yrow = jnp.sum(eye * ycol[i * hd:(i + 1) * hd], axis=0, keepdims=True)
                yblk = jnp.where(sub == b, yrow, yblk)
            y_s[pl.ds(row0, bsz), p * LANES:(p + 1) * LANES] = yblk
        return carry

    lax.fori_loop(0, tt, step, 0)

    y = y_s[...]
    inv = 1.0 / hd
    mean = _dot_x2(y, seg256) * inv
    dv = y - mean
    var = _dot_x2(dv * dv, seg256) * inv
    yn = dv * lax.rsqrt(var + RW_LN_EPS) * lnw_ref[...] + lnb_ref[...]
    bonus = _dot_x2(r * k2 * rk_ref[...], seg256) * v
    o_ref[...] = (yn + bonus) * g


def _rwkv(c, ops, tt, bsz):
    rows = c.shape[0]
    blk = tt * bsz
    vec_spec = _full((1, RW_WIDTH))
    pad_spec = _full((LANES, RW_WIDTH))
    n_inst = (RW_WIDTH // LANES) * bsz
    return pl.pallas_call(
        functools.partial(_rwkv_body, tt=tt, bsz=bsz),
        grid=(rows // blk,),
        in_specs=[
            pl.BlockSpec((blk, RW_COLS), lambda i: (i, 0)),
            _full((1, RW_COLS)),
            pad_spec, pad_spec, pad_spec, pad_spec, pad_spec, pad_spec,
            vec_spec, vec_spec, vec_spec, vec_spec, vec_spec, vec_spec, vec_spec,
            _full((RW_WIDTH, RW_WIDTH)), _full((LANES, LANES)), _full((RW_HEAD_DIM, LANES)),
        ],
        out_specs=pl.BlockSpec((blk, RW_WIDTH), lambda i: (i, 0)),
        out_shape=jax.ShapeDtypeStruct((rows, RW_WIDTH), F32),
        scratch_shapes=[
            pltpu.VMEM((bsz, RW_COLS), F32),
            pltpu.VMEM((n_inst * RW_HEAD_DIM, LANES), F32),
        ] + [pltpu.VMEM((blk, RW_WIDTH), F32)] * 7,
        compiler_params=_cparams(1),
        name="rwkv7",
    )(c, *ops)


def _rwkv_operands(mu, decay_up, decay_w0, aaa_up, aaa_a0, gate_up, k_k, k_a, r_k, ln_w, ln_b):
    def pad(w, off):
        return _wsplit(jnp.zeros((LANES, RW_WIDTH), F32).at[off:off + w.shape[0]].set(w))

    row = lambda t: t.reshape(1, RW_WIDTH)
    lane = jnp.arange(LANES)
    eye = (lane[None, :] % RW_HEAD_DIM == jnp.arange(RW_HEAD_DIM)[:, None]).astype(F32)
    return (mu.reshape(1, RW_COLS),
            *pad(decay_up, 0), *pad(aaa_up, RW_DECAY_RANK),
            *pad(gate_up, RW_DECAY_RANK + RW_AAA_RANK),
            row(decay_w0), row(aaa_a0), row(k_k), row(k_a), row(r_k), row(ln_w), row(ln_b),
            _seg_ones(RW_WIDTH, RW_HEAD_DIM), _seg_ones(LANES, RW_HEAD_DIM), eye)


def _outproj_body(x_ref, ya_ref, yb_ref, yc_ref, wa_ref, wb_ref, wc_ref, g_ref, wq_ref,
                  keys_ref, xn_ref, h_ref, st_ref):
    xn = (x_ref[...]
          + _dot(ya_ref[...].astype(BF16), wa_ref[...])
          + _dot(yb_ref[...].astype(BF16), wb_ref[...])
          + _dot(yc_ref[...].astype(BF16), wc_ref[...]))
    xn_ref[...] = xn
    ms = jnp.mean(xn * xn, axis=-1, keepdims=True)
    h = xn * lax.rsqrt(ms + NORM_EPS) * g_ref[...]
    h_ref[...] = h
    q = _dot(h.astype(BF16), wq_ref[...]).astype(BF16)
    for hm in range(2 * PEER_HEADS):
        st_ref[hm * N_KEYS:(hm + 1) * N_KEYS, :] = lax.dot_general(
            keys_ref[hm], q[:, hm * PEER_KEY_DIM:(hm + 1) * PEER_KEY_DIM],
            (((1,), (1,)), ((), ())), preferred_element_type=F32)


def _outproj(x, ya, yb, yc, wa, wb, wc, g, wq, keys, tl):
    bsz, seq, _ = x.shape
    nl = seq // tl
    n_rows = 2 * PEER_HEADS * N_KEYS
    tok_spec = pl.BlockSpec((None, tl, D_MODEL), lambda b, i: (b, i, 0))
    tm_spec = pl.BlockSpec((tl, SSM_WIDTH), lambda b, i: (i, b))
    return pl.pallas_call(
        _outproj_body,
        grid=(bsz, nl),
        in_specs=[
            tok_spec,
            pl.BlockSpec((None, tl, DA_WIDTH), lambda b, i: (b, i, 0)),
            tm_spec, tm_spec,
            _full((DA_WIDTH, D_MODEL)), _full((SSM_WIDTH, D_MODEL)), _full((RW_WIDTH, D_MODEL)),
            _full((1, D_MODEL)),
            _full((D_MODEL, n_rows)),
            _full((2 * PEER_HEADS, N_KEYS, PEER_KEY_DIM)),
        ],
        out_specs=[
            tok_spec, tok_spec,
            pl.BlockSpec((n_rows, tl), lambda b, i: (0, b * nl + i)),
        ],
        out_shape=[
            jax.ShapeDtypeStruct((bsz, seq, D_MODEL), F32),
            jax.ShapeDtypeStruct((bsz, seq, D_MODEL), F32),
            jax.ShapeDtypeStruct((n_rows, bsz * seq), F32),
        ],
        compiler_params=_cparams(2),
        name="outproj_peer_scores",
    )(x, ya, yb, yc, wa, wb, wc, g, wq, keys)


def _topk_rows(s, k, payload=None):
    n, width = s.shape
    rows = lax.broadcasted_iota(jnp.int32, s.shape, 0)
    out_rows = lax.broadcasted_iota(jnp.int32, (k, width), 0)
    vals = jnp.zeros((k, width), F32)
    picks = jnp.zeros((k, width), jnp.int32)
    for j in range(k):
        m = jnp.max(s, axis=0, keepdims=True)
        pos = jnp.min(jnp.where(s == m, rows, n), axis=0, keepdims=True)
        hit = rows == pos
        pick = pos if payload is None else jnp.max(jnp.where(hit, payload, -1), axis=0, keepdims=True)
        vals = jnp.where(out_rows == j, m, vals)
        picks = jnp.where(out_rows == j, pick, picks)
        s = jnp.where(hit, -jnp.inf, s)
    return vals, picks


def _topk_body(st_ref, idx_ref, gate_ref):
    idx_rows, gate_rows = [], []
    for h in range(PEER_HEADS):
        v1, i1 = _topk_rows(st_ref[(2 * h) * N_KEYS:(2 * h + 1) * N_KEYS, :], PEER_TOPK)
        v2, i2 = _topk_rows(st_ref[(2 * h + 1) * N_KEYS:(2 * h + 2) * N_KEYS, :], PEER_TOPK)
        cand_s = jnp.concatenate([v1[a:a + 1] + v2 for a in range(PEER_TOPK)], axis=0)
        cand_e = jnp.concatenate([i1[a:a + 1] * N_KEYS + i2 for a in range(PEER_TOPK)], axis=0)
        best, expert = _topk_rows(cand_s, PEER_TOPK, payload=cand_e)
        e = jnp.exp(best - best[0:1])
        gate_rows.append(e / jnp.sum(e, axis=0, keepdims=True))
        idx_rows.append(expert * PEER_ROW_SUBLANES)
    gate_ref[...] = jnp.concatenate(gate_rows, axis=0)
    idx_ref[...] = jnp.concatenate(idx_rows, axis=0).T


def _topk(scores_t, tb):
    t = scores_t.shape[1]
    return pl.pallas_call(
        _topk_body,
        grid=(t // tb,),
        in_specs=[pl.BlockSpec((scores_t.shape[0], tb), lambda i: (0, i))],
        out_specs=[
            pl.BlockSpec((tb, PEER_SLOTS), lambda i: (i, 0)),
            pl.BlockSpec((PEER_SLOTS, tb), lambda i: (0, i)),
        ],
        out_shape=[
            jax.ShapeDtypeStruct((t, PEER_SLOTS), jnp.int32),
            jax.ShapeDtypeStruct((PEER_SLOTS, t), F32),
        ],
        compiler_params=_cparams(1),
        name="peer_topk",
    )(scores_t)


def _pack_table(tab):
    tb = lax.bitcast_convert_type(tab.astype(BF16), jnp.uint16).astype(jnp.uint32)
    words = tb[:, :PEER_ROW_WORDS] | (tb[:, PEER_ROW_WORDS:] << 16)
    return words.reshape(tab.shape[0] * PEER_ROW_SUBLANES, LANES)


def _load_table(tab_hbm, tab_ref, sem):
    @pl.when(pl.program_id(0) == 0)
    def _():
        cp = pltpu.make_async_copy(tab_hbm, tab_ref, sem)
        cp.start()
        cp.wait()


def _expert_row(tab_ref, off):
    words = tab_ref[pl.ds(pl.multiple_of(off, PEER_ROW_SUBLANES), PEER_ROW_SUBLANES), :]
    lo = lax.bitcast_convert_type(words << 16, F32)
    hi = lax.bitcast_convert_type(words & jnp.uint32(0xFFFF0000), F32)
    return lo, hi


def _peer_act_body(idx_ref, h_ref, gate_ref, g4_ref, tab_hbm, coef_ref, tab_ref, sem, p_ref, ct_ref,
                   *, tb):
    _load_table(tab_hbm, tab_ref, sem)
    rs = PEER_ROW_SUBLANES
    lane = lax.broadcasted_iota(jnp.int32, (PEER_SLOTS, tb), 1)
    g4 = g4_ref[...]
    ct_ref[...] = jnp.zeros(ct_ref.shape, F32)

    def token(tt, carry):
        base = pl.multiple_of(tt * SUBLANES, SUBLANES)
        h_lo = h_ref[pl.ds(base, rs), :]
        h_hi = h_ref[pl.ds(base + rs, rs), :]
        for k in range(PEER_SLOTS):
            lo, hi = _expert_row(tab_ref, idx_ref[tt, k])
            p_ref[k * rs:(k + 1) * rs, :] = lo * h_lo + hi * h_hi
        hi_p, lo_p = _split(p_ref[...])
        part = _dot(g4, hi_p) + _dot(g4, lo_p)
        act = jnp.sum(part, axis=1, keepdims=True)
        ct_ref[...] = jnp.where(lane == tt, _gelu(act) * gate_ref[...], ct_ref[...])
        return carry

    lax.fori_loop(0, tb, token, 0)
    coef_ref[...] = ct_ref[...].T


def _peer_act(idx, h_rows, gate_t, tab, tb):
    t = idx.shape[0]
    g4 = (jnp.arange(PEER_SLOTS * PEER_ROW_SUBLANES)[None, :] // PEER_ROW_SUBLANES
          == jnp.arange(PEER_SLOTS)[:, None]).astype(BF16)
    return pl.pallas_call(
        functools.partial(_peer_act_body, tb=tb),
        grid=(t // tb,),
        in_specs=[
            pl.BlockSpec((tb, PEER_SLOTS), lambda i: (i, 0), memory_space=pltpu.SMEM),
            pl.BlockSpec((tb * SUBLANES, LANES), lambda i: (i, 0)),
            pl.BlockSpec((PEER_SLOTS, tb), lambda i: (0, i)),
            _full(g4.shape),
            pl.BlockSpec(memory_space=pl.ANY),
        ],
        out_specs=pl.BlockSpec((tb, PEER_SLOTS), lambda i: (i, 0)),
        out_shape=jax.ShapeDtypeStruct((t, PEER_SLOTS), F32),
        scratch_shapes=[
            pltpu.VMEM(tab.shape, jnp.uint32),
            pltpu.SemaphoreType.DMA,
            pltpu.VMEM((PEER_SLOTS * PEER_ROW_SUBLANES, LANES), F32),
            pltpu.VMEM((PEER_SLOTS, tb), F32),
        ],
        compiler_params=_cparams(1),
        name="peer_act",
    )(idx, h_rows, gate_t, g4, tab)


def _peer_out_body(idx_ref, coef_ref, x_ref, tab_hbm, o_ref, tab_ref, sem, *, tb):
    _load_table(tab_hbm, tab_ref, sem)
    rs = PEER_ROW_SUBLANES
    n_acc = 4

    def token(tt, carry):
        base = pl.multiple_of(tt * SUBLANES, SUBLANES)
        acc_lo = [jnp.zeros((rs, LANES), F32) for _ in range(n_acc)]
        acc_hi = [jnp.zeros((rs, LANES), F32) for _ in range(n_acc)]
        for k in range(PEER_SLOTS):
            lo, hi = _expert_row(tab_ref, idx_ref[tt, k])
            cf = coef_ref[tt, k]
            acc_lo[k % n_acc] = acc_lo[k % n_acc] + cf * lo
            acc_hi[k % n_acc] = acc_hi[k % n_acc] + cf * hi
        o_ref[pl.ds(base, rs), :] = x_ref[pl.ds(base, rs), :] + sum(acc_lo[1:], acc_lo[0])
        o_ref[pl.ds(base + rs, rs), :] = x_ref[pl.ds(base + rs, rs), :] + sum(acc_hi[1:], acc_hi[0])
        return carry

    lax.fori_loop(0, tb, token, 0)


def _peer_out(idx, coef, x_rows, tab, tb):
    t = idx.shape[0]
    smem_spec = pl.BlockSpec((tb, PEER_SLOTS), lambda i: (i, 0), memory_space=pltpu.SMEM)
    row_spec = pl.BlockSpec((tb * SUBLANES, LANES), lambda i: (i, 0))
    return pl.pallas_call(
        functools.partial(_peer_out_body, tb=tb),
        grid=(t // tb,),
        in_specs=[smem_spec, smem_spec, row_spec, pl.BlockSpec(memory_space=pl.ANY)],
        out_specs=row_spec,
        out_shape=jax.ShapeDtypeStruct(x_rows.shape, F32),
        scratch_shapes=[pltpu.VMEM(tab.shape, jnp.uint32), pltpu.SemaphoreType.DMA],
        compiler_params=_cparams(1),
        name="peer_out",
    )(idx, coef, x_rows, tab)


def _tiles(bsz, seq):
    tl = min(seq, 512)
    tq = min(seq, 256)
    tt_s5 = min(seq, 128)
    tt_rw = min(seq, 64)
    tb = 128
    return tl, tq, tt_s5, tt_rw, tb


def kernel(x, positions, norm_mix, w_in, da_q_norm, da_k_norm, da_lambda, da_subln, ssm_lambda_re, ssm_lambda_im, ssm_b_re, ssm_b_im, ssm_c_re, ssm_c_im, ssm_d, ssm_log_dt, ssm_w_glu, ssm_b_glu, ssm_out_norm, rw_mu, rw_decay_up, rw_decay_w0, rw_aaa_up, rw_aaa_a0, rw_gate_up, rw_k_k, rw_k_a, rw_r_k, rw_ln_w, rw_ln_b, w_out, norm_ffn, peer_w_q, peer_sub_keys, peer_u, peer_v):
    bsz, seq, _ = x.shape
    depth = w_in.shape[0]
    t = bsz * seq
    tl, tq, tt_s5, tt_rw, tb = _tiles(bsz, seq)
    rope = _rope_tables(positions, tl)

    for l in range(depth):
        w = w_in[l].astype(BF16)
        cols_a, cols_b, cols_c = _inproj(
            x, norm_mix[l].reshape(1, D_MODEL), w[:, :DA_COLS],
            w[:, DA_COLS:DA_COLS + SSM_WIDTH], w[:, DA_COLS + SSM_WIDTH:], tl)

        q, k, v = _attn_prep(cols_a.reshape(t, DA_COLS), rope,
                             da_q_norm[l].reshape(1, DA_WIDTH), da_k_norm[l].reshape(1, DA_WIDTH), tl)
        lam_init = 0.8 - 0.6 * math.exp(-0.3 * l)
        lam_pad = (jnp.zeros((SUBLANES, LANES), F32)
                   .at[0:4, 0:DA_HEAD_DIM].set(da_lambda[l]).at[4, :].set(lam_init))
        y_a = _attention(q.reshape(bsz, seq, DA_WIDTH), k.reshape(bsz, seq, DA_WIDTH),
                         v.reshape(bsz, seq, DA_WIDTH), lam_pad, da_subln[l].reshape(1, DA_V_DIM), tq)

        s5_ops = _s5_operands(ssm_lambda_re[l], ssm_lambda_im[l], ssm_b_re[l], ssm_b_im[l],
                              ssm_c_re[l], ssm_c_im[l], ssm_d[l], ssm_log_dt[l], ssm_w_glu[l],
                              ssm_b_glu[l], ssm_out_norm[l])
        y_b = _s5(cols_b.reshape(t, SSM_WIDTH), s5_ops, tt_s5, bsz)

        rw_ops = _rwkv_operands(rw_mu[l], rw_decay_up[l], rw_decay_w0[l], rw_aaa_up[l], rw_aaa_a0[l],
                                rw_gate_up[l], rw_k_k[l], rw_k_a[l], rw_r_k[l], rw_ln_w[l], rw_ln_b[l])
        y_c = _rwkv(cols_c.reshape(t, RW_COLS), rw_ops, tt_rw, bsz)

        wo = w_out[l].astype(BF16)
        keys = peer_sub_keys[l].reshape(2 * PEER_HEADS, N_KEYS, PEER_KEY_DIM).astype(BF16)
        x_mid, h_ffn, scores_t = _outproj(
            x, y_a, y_b.reshape(seq, bsz * SSM_WIDTH), y_c.reshape(seq, bsz * RW_WIDTH),
            wo[:DA_WIDTH], wo[DA_WIDTH:DA_WIDTH + SSM_WIDTH], wo[DA_WIDTH + SSM_WIDTH:],
            norm_ffn[l].reshape(1, D_MODEL), peer_w_q[l].astype(BF16), keys, tl)

        idx, gate_t = _topk(scores_t, tb)
        coef = _peer_act(idx, h_ffn.reshape(t * SUBLANES, LANES), gate_t, _pack_table(peer_u[l]), tb)
        x = _peer_out(idx, coef, x_mid.reshape(t * SUBLANES, LANES),
                      _pack_table(peer_v[l]), tb).reshape(bsz, seq, D_MODEL)
    return x
```

```python
import functools
import math

import jax
import jax.numpy as jnp
from jax import lax
from jax.experimental import pallas as pl
from jax.experimental.pallas import tpu as pltpu

D_MODEL = 1024
CHUNK = 64
NORM_EPS = 1e-6

DA_HEADS = 4
DA_HEAD_DIM = 64
DA_V_DIM = 128
DA_WIDTH = 512
ROT_DIM = 16
ROPE_THETA = 500000.0

SSM_GROUP = 16
SSM_GROUPS = 16
SSM_WIDTH = 256
SSM_STATE = 64

RW_HEADS = 4
RW_HEAD_DIM = 64
RW_WIDTH = 256
RW_DECAY_RANK = 32
RW_AAA_RANK = 32
RW_GATE_RANK = 64
RW_LN_EPS = 64e-5
RW_COLS = 896

DA_COLS = 1536

PEER_HEADS = 8
N_KEYS = 128
PEER_KEY_DIM = 128
PEER_TOPK = 16
PEER_SLOTS = PEER_HEADS * PEER_TOPK
PEER_ROW_WORDS = D_MODEL // 2
PEER_ROW_SUBLANES = PEER_ROW_WORDS // 128

LANES = 128
SUBLANES = 8
VMEM_LIMIT_BYTES = 56 * 1024 * 1024

F32 = jnp.float32
BF16 = jnp.bfloat16


def _cparams(n_axes, vmem=VMEM_LIMIT_BYTES):
    return pltpu.CompilerParams(
        dimension_semantics=("arbitrary",) * n_axes, vmem_limit_bytes=vmem)


def _dot(a, b):
    return jnp.dot(a, b, preferred_element_type=F32)


def _split(x):
    hi = x.astype(BF16)
    lo = (x - hi.astype(F32)).astype(BF16)
    return hi, lo


def _dot_x2(x, w):
    hi, lo = _split(x)
    return _dot(hi, w) + _dot(lo, w)


def _dot_x2k(x, w_stacked):
    hi, lo = _split(x)
    return _dot(jnp.concatenate([hi, lo], axis=1), w_stacked)


def _dot_x3(x, w_hi, w_lo):
    hi, lo = _split(x)
    return _dot(hi, w_hi) + _dot(lo, w_hi) + _dot(hi, w_lo)


def _wsplit(w):
    hi = w.astype(BF16)
    lo = (w - hi.astype(F32)).astype(BF16)
    return hi, lo


def _sigmoid(x):
    return 1.0 / (1.0 + jnp.exp(-x))


def _gelu(x):
    return 0.5 * x * (1.0 + lax.erf(x * (1.0 / math.sqrt(2.0))))


def _seg_ones(n, seg):
    r = jnp.arange(n) // seg
    return (r[:, None] == r[None, :]).astype(BF16)


def _full(shape):
    nd = len(shape)
    return pl.BlockSpec(shape, lambda *_: (0,) * nd)


def _inproj_body(x_ref, g_ref, wa_ref, wb_ref, wc_ref, a_ref, b_ref, c_ref):
    x = x_ref[...]
    ms = jnp.mean(x * x, axis=-1, keepdims=True)
    h = (x * lax.rsqrt(ms + NORM_EPS) * g_ref[...]).astype(BF16)
    a_ref[...] = _dot(h, wa_ref[...])
    b_ref[...] = _dot(h, wb_ref[...])
    c_ref[...] = _dot(h, wc_ref[...])


def _inproj(x, g, wa, wb, wc, tl):
    bsz, seq, _ = x.shape
    return pl.pallas_call(
        _inproj_body,
        grid=(bsz, seq // tl),
        in_specs=[
            pl.BlockSpec((None, tl, D_MODEL), lambda b, i: (b, i, 0)),
            _full((1, D_MODEL)),
            _full((D_MODEL, DA_COLS)),
            _full((D_MODEL, SSM_WIDTH)),
            _full((D_MODEL, RW_COLS)),
        ],
        out_specs=[
            pl.BlockSpec((None, tl, DA_COLS), lambda b, i: (b, i, 0)),
            pl.BlockSpec((tl, SSM_WIDTH), lambda b, i: (i, b)),
            pl.BlockSpec((tl, RW_COLS), lambda b, i: (i, b)),
        ],
        out_shape=[
            jax.ShapeDtypeStruct((bsz, seq, DA_COLS), F32),
            jax.ShapeDtypeStruct((seq, bsz * SSM_WIDTH), F32),
            jax.ShapeDtypeStruct((seq, bsz * RW_COLS), F32),
        ],
        compiler_params=_cparams(2),
        name="inproj",
    )(x, g, wa, wb, wc)


def _rope_body(pos_ref, invf_ref, c_ref, s1_ref, s2_ref):
    ang = pos_ref[...].astype(F32) * invf_ref[...]
    c, s = jnp.cos(ang), jnp.sin(ang)
    lm = lax.broadcasted_iota(jnp.int32, ang.shape, 1) % DA_HEAD_DIM
    half = ROT_DIM // 2
    c_ref[...] = jnp.where(lm < ROT_DIM, c, 1.0)
    s1_ref[...] = jnp.where(lm < half, -s, 0.0)
    s2_ref[...] = jnp.where((lm >= half) & (lm < ROT_DIM), s, 0.0)


def _rope_tables(positions, tm):
    t = positions.size
    half = ROT_DIM // 2
    inv_freq = ROPE_THETA ** (-jnp.arange(0, ROT_DIM, 2, dtype=F32) / ROT_DIM)
    invf = inv_freq[jnp.arange(LANES) % half][None, :]
    pos = jnp.broadcast_to(positions.reshape(t, 1), (t, LANES))
    spec = pl.BlockSpec((tm, LANES), lambda i: (i, 0))
    return pl.pallas_call(
        _rope_body,
        grid=(t // tm,),
        in_specs=[spec, _full((1, LANES))],
        out_specs=[spec, spec, spec],
        out_shape=[jax.ShapeDtypeStruct((t, LANES), F32)] * 3,
        compiler_params=_cparams(1),
        name="rope_tables",
    )(pos, invf)


def _attn_prep_body(a_ref, c_ref, s1_ref, s2_ref, qg_ref, kg_ref, seg_ref,
                    q_ref, k_ref, v_ref):
    reps = DA_WIDTH // LANES
    c = jnp.tile(c_ref[...], (1, reps))
    s1 = jnp.tile(s1_ref[...], (1, reps))
    s2 = jnp.tile(s2_ref[...], (1, reps))
    seg = seg_ref[...]
    half = ROT_DIM // 2

    def norm_rope(t, gain):
        ms = _dot_x2(t * t, seg) * (1.0 / DA_HEAD_DIM)
        tn = t * lax.rsqrt(ms + NORM_EPS) * gain
        up = pltpu.roll(tn, DA_WIDTH - half, axis=1)
        dn = pltpu.roll(tn, half, axis=1)
        return tn * c + up * s1 + dn * s2

    q = norm_rope(a_ref[:, 0:DA_WIDTH], qg_ref[...])
    k = norm_rope(a_ref[:, DA_WIDTH:2 * DA_WIDTH], kg_ref[...])
    q_ref[...] = (q * (DA_HEAD_DIM ** -0.5)).astype(BF16)
    k_ref[...] = k.astype(BF16)
    v_ref[...] = a_ref[:, 2 * DA_WIDTH:3 * DA_WIDTH].astype(BF16)


def _attn_prep(cols_a, rope, qg, kg, tm):
    t = cols_a.shape[0]
    c, s1, s2 = rope
    lane_spec = pl.BlockSpec((tm, LANES), lambda i: (i, 0))
    out_spec = pl.BlockSpec((tm, DA_WIDTH), lambda i: (i, 0))
    return pl.pallas_call(
        _attn_prep_body,
        grid=(t // tm,),
        in_specs=[
            pl.BlockSpec((tm, DA_COLS), lambda i: (i, 0)),
            lane_spec, lane_spec, lane_spec,
            _full((1, DA_WIDTH)), _full((1, DA_WIDTH)),
            _full((DA_WIDTH, DA_WIDTH)),
        ],
        out_specs=[out_spec, out_spec, out_spec],
        out_shape=[jax.ShapeDtypeStruct((t, DA_WIDTH), BF16)] * 3,
        compiler_params=_cparams(1),
        name="attn_prep",
    )(cols_a, c, s1, s2, qg, kg, _seg_ones(DA_WIDTH, DA_HEAD_DIM))


def _attn_body(q_ref, k_ref, v_ref, lam_ref, sub_ref, o_ref, m_ref, l_ref, acc_ref, s_ref, p_ref,
               *, tq, tw):
    qi = pl.program_id(2)
    q = q_ref[...]
    lane = lax.broadcasted_iota(jnp.int32, q.shape, 1)
    zero = jnp.zeros_like(q)
    qmaps = (jnp.where(lane < DA_HEAD_DIM, q, zero), jnp.where(lane >= DA_HEAD_DIM, q, zero))

    m_ref[...] = jnp.full(m_ref.shape, -jnp.inf, F32)
    l_ref[...] = jnp.zeros(l_ref.shape, F32)
    acc_ref[...] = jnp.zeros(acc_ref.shape, F32)

    def span(c0, masked):
        c0 = pl.multiple_of(c0, tw)
        k = k_ref[pl.ds(c0, tw), :]
        v = v_ref[pl.ds(c0, tw), :]
        if masked:
            row_chunk = (lax.broadcasted_iota(jnp.int32, (tq, tw), 0) + tq) // CHUNK
            col_chunk = (lax.broadcasted_iota(jnp.int32, (tq, tw), 1) + (c0 - qi * tq + tq)) // CHUNK
            visible = col_chunk <= row_chunk
        for m in range(2):
            s = lax.dot_general(qmaps[m], k, (((1,), (1,)), ((), ())),
                                preferred_element_type=F32)
            s_ref[m] = jnp.where(visible, s, -jnp.inf) if masked else s
        for m in range(2):
            s = s_ref[m]
            m_old = m_ref[m]
            m_new = jnp.maximum(m_old, jnp.max(s, axis=1, keepdims=True))
            alpha = jnp.exp(m_old - m_new)
            p = jnp.exp(s - jnp.tile(m_new, (1, tw // LANES)))
            p_ref[m] = p.astype(BF16)
            l_ref[m] = alpha * l_ref[m] + jnp.sum(p, axis=1, keepdims=True)
            acc_ref[m] = alpha * acc_ref[m]
            m_ref[m] = m_new
        for m in range(2):
            acc_ref[m] += _dot(p_ref[m], v)

    def full_span(c, carry):
        span(c * tw, False)
        return carry

    n_full = qi // (tw // tq)
    lax.fori_loop(0, n_full, full_span, 0)
    span(n_full * tw, True)

    lam = lam_ref[...]
    lam_full = (jnp.exp(jnp.sum(lam[0:1] * lam[1:2], axis=1, keepdims=True))
                - jnp.exp(jnp.sum(lam[2:3] * lam[3:4], axis=1, keepdims=True))
                + lam[4:5, 0:1])
    lam_init = lam[4:5, 0:1]
    o = acc_ref[0] / l_ref[0] - lam_full * (acc_ref[1] / l_ref[1])
    ms = jnp.mean(o * o, axis=-1, keepdims=True)
    o_ref[...] = o * lax.rsqrt(ms + NORM_EPS) * sub_ref[...] * (1.0 - lam_init)


def _attention(q, k, v, lam_pad, subln, tq):
    bsz, seq, _ = q.shape
    tw = min(2 * tq, seq)
    kv_spec = pl.BlockSpec((None, seq, DA_V_DIM), lambda b, h, i: (b, 0, h))
    return pl.pallas_call(
        functools.partial(_attn_body, tq=tq, tw=tw),
        grid=(bsz, DA_HEADS, seq // tq),
        in_specs=[
            pl.BlockSpec((None, tq, DA_V_DIM), lambda b, h, i: (b, i, h)),
            kv_spec, kv_spec,
            _full((SUBLANES, LANES)),
            _full((1, DA_V_DIM)),
        ],
        out_specs=pl.BlockSpec((None, tq, DA_V_DIM), lambda b, h, i: (b, i, h)),
        out_shape=jax.ShapeDtypeStruct((bsz, seq, DA_WIDTH), F32),
        scratch_shapes=[
            pltpu.VMEM((2, tq, LANES), F32),
            pltpu.VMEM((2, tq, LANES), F32),
            pltpu.VMEM((2, tq, DA_V_DIM), F32),
            pltpu.VMEM((2, tq, tw), F32),
            pltpu.VMEM((2, tq, tw), BF16),
        ],
        compiler_params=_cparams(3),
        name="diff_attention",
    )(q, k, v, lam_pad, subln)


def _s5_body(u_ref, bhi_ref, blo_ref, are_ref, aim_ref, chi_ref, clo_ref, d_ref,
             whi_ref, wlo_ref, bg_ref, g_ref, o_ref, st_ref, x_ref, *, tt, bsz):
    ns = SSM_GROUPS * SSM_STATE

    @pl.when(pl.program_id(0) == 0)
    def _():
        st_ref[...] = jnp.zeros(st_ref.shape, F32)

    u = u_ref[...]
    x_ref[...] = _dot_x3(u, bhi_ref[...], blo_ref[...])
    a_re = jnp.broadcast_to(are_ref[...], (bsz, ns))
    a_im = jnp.broadcast_to(aim_ref[...], (bsz, ns))

    def step(t, carry):
        xr, xi = carry
        r = pl.multiple_of(t * bsz, bsz)
        nr = a_re * xr - a_im * xi + x_ref[pl.ds(r, bsz), 0:ns]
        ni = a_re * xi + a_im * xr + x_ref[pl.ds(r, bsz), ns:2 * ns]
        x_ref[pl.ds(r, bsz), 0:ns] = nr
        x_ref[pl.ds(r, bsz), ns:2 * ns] = ni
        return nr, ni

    xr, xi = lax.fori_loop(0, tt, step, (st_ref[:, 0:ns], st_ref[:, ns:2 * ns]))
    st_ref[:, 0:ns] = xr
    st_ref[:, ns:2 * ns] = xi

    y = _dot_x3(x_ref[...], chi_ref[...], clo_ref[...]) + d_ref[...] * u
    yg = _gelu(y)
    z = _dot_x3(yg, whi_ref[...], wlo_ref[...]) + bg_ref[...]
    out = yg * _sigmoid(z)
    ms = jnp.mean(out * out, axis=-1, keepdims=True)
    o_ref[...] = out * lax.rsqrt(ms + NORM_EPS) * g_ref[...]


def _s5(u, ops, tt, bsz):
    rows = u.shape[0]
    ns2 = 2 * SSM_GROUPS * SSM_STATE
    blk = tt * bsz
    io_spec = pl.BlockSpec((blk, SSM_WIDTH), lambda i: (i, 0))
    return pl.pallas_call(
        functools.partial(_s5_body, tt=tt, bsz=bsz),
        grid=(rows // blk,),
        in_specs=[
            io_spec,
            _full((SSM_WIDTH, ns2)), _full((SSM_WIDTH, ns2)),
            _full((1, ns2 // 2)), _full((1, ns2 // 2)),
            _full((ns2, SSM_WIDTH)), _full((ns2, SSM_WIDTH)),
            _full((1, SSM_WIDTH)),
            _full((SSM_WIDTH, SSM_WIDTH)), _full((SSM_WIDTH, SSM_WIDTH)),
            _full((1, SSM_WIDTH)), _full((1, SSM_WIDTH)),
        ],
        out_specs=io_spec,
        out_shape=jax.ShapeDtypeStruct((rows, SSM_WIDTH), F32),
        scratch_shapes=[
            pltpu.VMEM((bsz, ns2), F32),
            pltpu.VMEM((blk, ns2), F32),
        ],
        compiler_params=_cparams(1),
        name="s5",
    )(u, *ops)


def _s5_operands(lam_re, lam_im, b_re, b_im, c_re, c_im, d_skip, log_dt, w_glu, b_glu, out_norm):
    lr = jnp.minimum(lam_re, -1e-4)
    li = lam_im
    step = jnp.exp(log_dt)[:, None]
    mag = jnp.exp(lr * step)
    abar_re = mag * jnp.cos(li * step)
    abar_im = mag * jnp.sin(li * step)
    den = lr * lr + li * li
    coef_re = ((abar_re - 1.0) * lr + abar_im * li) / den
    coef_im = (abar_im * lr - (abar_re - 1.0) * li) / den
    bbar_re = coef_re[..., None] * b_re - coef_im[..., None] * b_im
    bbar_im = coef_re[..., None] * b_im + coef_im[..., None] * b_re
    eye = jnp.eye(SSM_GROUPS, dtype=F32)
    ns = SSM_GROUPS * SSM_STATE

    def in_mat(bb):
        return jnp.einsum('gph,gk->ghkp', bb, eye).reshape(SSM_WIDTH, ns)

    def out_mat(cc):
        return jnp.einsum('ghp,gk->kpgh', cc, eye).reshape(ns, SSM_WIDTH)

    bmat = jnp.concatenate([in_mat(bbar_re), in_mat(bbar_im)], axis=1)
    cmat = jnp.concatenate([out_mat(c_re), -out_mat(c_im)], axis=0)
    return (*_wsplit(bmat), abar_re.reshape(1, ns), abar_im.reshape(1, ns), *_wsplit(cmat),
            d_skip.reshape(1, SSM_WIDTH), *_wsplit(w_glu), b_glu.reshape(1, SSM_WIDTH),
            out_norm.reshape(1, SSM_WIDTH))


def _rwkv_body(c_ref, mu_ref, wdh_ref, wdl_ref, wah_ref, wal_ref, wgh_ref, wgl_ref,
               w0_ref, a0_ref, kk_ref, ka_ref, rk_ref, lnw_ref, lnb_ref,
               seg256_ref, seg128_ref, eye_ref,
               o_ref, prev_ref, st_ref, r_s, w_s, k_s, v_s, a_s, b_s, y_s, *, tt, bsz):
    n_pairs = RW_WIDTH // LANES
    n_inst = n_pairs * bsz
    hd = RW_HEAD_DIM

    @pl.when(pl.program_id(0) == 0)
    def _():
        prev_ref[...] = jnp.zeros(prev_ref.shape, F32)
        st_ref[...] = jnp.zeros(st_ref.shape, F32)

    c = c_ref[...]
    rows = c.shape[0]
    prev = jnp.concatenate([prev_ref[...], c[0:rows - bsz]], axis=0)
    prev_ref[...] = c[rows - bsz:rows]
    cs = c + (prev - c) * mu_ref[...]
    r = cs[:, 0:RW_WIDTH]
    k = cs[:, RW_WIDTH:2 * RW_WIDTH]
    v = cs[:, 2 * RW_WIDTH:3 * RW_WIDTH]
    x3 = cs[:, 3 * RW_WIDTH:RW_COLS]

    seg256 = seg256_ref[...]
    wdec = w0_ref[...] + _dot_x3(jnp.tanh(x3), wdh_ref[...], wdl_ref[...])
    z = -wdec
    softplus = jnp.maximum(z, 0.0) + jnp.log(1.0 + jnp.exp(-jnp.abs(z)))
    decay = jnp.exp(-jnp.exp(-softplus - 0.5))
    a = _sigmoid(a0_ref[...] + _dot_x3(x3, wah_ref[...], wal_ref[...]))
    g = _dot_x3(_sigmoid(x3), wgh_ref[...], wgl_ref[...])
    kk = k * kk_ref[...]
    kk = kk / jnp.maximum(jnp.sqrt(_dot_x2(kk * kk, seg256)), 1e-12)
    k2 = k * (1.0 + (a - 1.0) * ka_ref[...])
    r_s[...] = r
    w_s[...] = decay
    k_s[...] = k2
    v_s[...] = v
    a_s[...] = -kk
    b_s[...] = kk * a

    seg128x2 = seg128_ref[...]
    seg128 = seg128x2[0:LANES]
    eye = eye_ref[...]
    sub = lax.broadcasted_iota(jnp.int32, (bsz, LANES), 0)

    def step(t, carry):
        row0 = pl.multiple_of(t * bsz, bsz)
        blk = [ref[pl.ds(row0, bsz), :] for ref in (r_s, w_s, k_s, v_s, a_s, b_s)]

        def vec(q, inst):
            p, b = divmod(inst, bsz)
            return blk[q][b:b + 1, p * LANES:(p + 1) * LANES]

        states = [st_ref[i * hd:(i + 1) * hd, :] for i in range(n_inst)]
        lhs_a = jnp.concatenate([states[i] * vec(4, i) for i in range(n_inst)], axis=0)
        lhs_v = jnp.concatenate([eye * vec(3, i) for i in range(n_inst)], axis=0)
        sa = _dot_x2k(lhs_a, seg128x2)
        vcol = _dot_x2k(lhs_v, seg128x2)
        new = []
        for i in range(n_inst):
            sl = slice(i * hd, (i + 1) * hd)
            new.append(states[i] * vec(1, i) + sa[sl] * vec(5, i) + vcol[sl] * vec(2, i))
        lhs_y = jnp.concatenate([new[i] * vec(0, i) for i in range(n_inst)], axis=0)
        ycol = _dot(lhs_y.astype(BF16), seg128)
        for i in range(n_inst):
            st_ref[i * hd:(i + 1) * hd, :] = new[i]
        for p in range(n_pairs):
            yblk = jnp.zeros((bsz, LANES), F32)
            for b in range(bsz):
                i = p * bsz + b
                yrow = jnp.sum(eye * ycol[i * hd:(i + 1) * hd], axis=0, keepdims=True)
                yblk = jnp.where(sub == b, yrow, yblk)
            y_s[pl.ds(row0, bsz), p * LANES:(p + 1) * LANES] = yblk
        return carry

    lax.fori_loop(0, tt, step, 0)

    y = y_s[...]
    inv = 1.0 / hd
    mean = _dot_x2(y, seg256) * inv
    dv = y - mean
    var = _dot_x2(dv * dv, seg256) * inv
    yn = dv * lax.rsqrt(var + RW_LN_EPS) * lnw_ref[...] + lnb_ref[...]
    bonus = _dot_x2(r * k2 * rk_ref[...], seg256) * v
    o_ref[...] = (yn + bonus) * g


def _rwkv(c, ops, tt, bsz):
    rows = c.shape[0]
    blk = tt * bsz
    vec_spec = _full((1, RW_WIDTH))
    pad_spec = _full((LANES, RW_WIDTH))
    n_inst = (RW_WIDTH // LANES) * bsz
    return pl.pallas_call(
        functools.partial(_rwkv_body, tt=tt, bsz=bsz),
        grid=(rows // blk,),
        in_specs=[
            pl.BlockSpec((blk, RW_COLS), lambda i: (i, 0)),
            _full((1, RW_COLS)),
            pad_spec, pad_spec, pad_spec, pad_spec, pad_spec, pad_spec,
            vec_spec, vec_spec, vec_spec, vec_spec, vec_spec, vec_spec, vec_spec,
            _full((RW_WIDTH, RW_WIDTH)), _full((2 * LANES, LANES)), _full((RW_HEAD_DIM, LANES)),
        ],
        out_specs=pl.BlockSpec((blk, RW_WIDTH), lambda i: (i, 0)),
        out_shape=jax.ShapeDtypeStruct((rows, RW_WIDTH), F32),
        scratch_shapes=[
            pltpu.VMEM((bsz, RW_COLS), F32),
            pltpu.VMEM((n_inst * RW_HEAD_DIM, LANES), F32),
        ] + [pltpu.VMEM((blk, RW_WIDTH), F32)] * 7,
        compiler_params=_cparams(1),
        name="rwkv7",
    )(c, *ops)


def _rwkv_operands(mu, decay_up, decay_w0, aaa_up, aaa_a0, gate_up, k_k, k_a, r_k, ln_w, ln_b):
    def pad(w, off):
        return _wsplit(jnp.zeros((LANES, RW_WIDTH), F32).at[off:off + w.shape[0]].set(w))

    row = lambda t: t.reshape(1, RW_WIDTH)
    lane = jnp.arange(LANES)
    eye = (lane[None, :] % RW_HEAD_DIM == jnp.arange(RW_HEAD_DIM)[:, None]).astype(F32)
    return (mu.reshape(1, RW_COLS),
            *pad(decay_up, 0), *pad(aaa_up, RW_DECAY_RANK),
            *pad(gate_up, RW_DECAY_RANK + RW_AAA_RANK),
            row(decay_w0), row(aaa_a0), row(k_k), row(k_a), row(r_k), row(ln_w), row(ln_b),
            _seg_ones(RW_WIDTH, RW_HEAD_DIM),
            jnp.concatenate([_seg_ones(LANES, RW_HEAD_DIM)] * 2, axis=0), eye)


def _outproj_body(x_ref, ya_ref, yb_ref, yc_ref, wa_ref, wb_ref, wc_ref, g_ref, wq_ref,
                  keys_ref, xn_ref, h_ref, st_ref):
    xn = (x_ref[...]
          + _dot(ya_ref[...].astype(BF16), wa_ref[...])
          + _dot(yb_ref[...].astype(BF16), wb_ref[...])
          + _dot(yc_ref[...].astype(BF16), wc_ref[...]))
    xn_ref[...] = xn
    ms = jnp.mean(xn * xn, axis=-1, keepdims=True)
    h = xn * lax.rsqrt(ms + NORM_EPS) * g_ref[...]
    h_ref[...] = h
    q = _dot(h.astype(BF16), wq_ref[...]).astype(BF16)
    for hm in range(2 * PEER_HEADS):
        st_ref[hm * N_KEYS:(hm + 1) * N_KEYS, :] = lax.dot_general(
            keys_ref[hm], q[:, hm * PEER_KEY_DIM:(hm + 1) * PEER_KEY_DIM],
            (((1,), (1,)), ((), ())), preferred_element_type=F32)


def _outproj(x, ya, yb, yc, wa, wb, wc, g, wq, keys, tl):
    bsz, seq, _ = x.shape
    nl = seq // tl
    n_rows = 2 * PEER_HEADS * N_KEYS
    tok_spec = pl.BlockSpec((None, tl, D_MODEL), lambda b, i: (b, i, 0))
    tm_spec = pl.BlockSpec((tl, SSM_WIDTH), lambda b, i: (i, b))
    return pl.pallas_call(
        _outproj_body,
        grid=(bsz, nl),
        in_specs=[
            tok_spec,
            pl.BlockSpec((None, tl, DA_WIDTH), lambda b, i: (b, i, 0)),
            tm_spec, tm_spec,
            _full((DA_WIDTH, D_MODEL)), _full((SSM_WIDTH, D_MODEL)), _full((RW_WIDTH, D_MODEL)),
            _full((1, D_MODEL)),
            _full((D_MODEL, n_rows)),
            _full((2 * PEER_HEADS, N_KEYS, PEER_KEY_DIM)),
        ],
        out_specs=[
            tok_spec, tok_spec,
            pl.BlockSpec((n_rows, tl), lambda b, i: (0, b * nl + i)),
        ],
        out_shape=[
            jax.ShapeDtypeStruct((bsz, seq, D_MODEL), F32),
            jax.ShapeDtypeStruct((bsz, seq, D_MODEL), F32),
            jax.ShapeDtypeStruct((n_rows, bsz * seq), F32),
        ],
        compiler_params=_cparams(2),
        name="outproj_peer_scores",
    )(x, ya, yb, yc, wa, wb, wc, g, wq, keys)


def _topk_rows(s, k, payload=None):
    n, width = s.shape
    rows = lax.broadcasted_iota(jnp.int32, s.shape, 0)
    out_rows = lax.broadcasted_iota(jnp.int32, (k, width), 0)
    vals = jnp.zeros((k, width), F32)
    picks = jnp.zeros((k, width), jnp.int32)
    for j in range(k):
        m = jnp.max(s, axis=0, keepdims=True)
        pos = jnp.min(jnp.where(s == m, rows, n), axis=0, keepdims=True)
        hit = rows == pos
        pick = pos if payload is None else jnp.max(jnp.where(hit, payload, -1), axis=0, keepdims=True)
        vals = jnp.where(out_rows == j, m, vals)
        picks = jnp.where(out_rows == j, pick, picks)
        s = jnp.where(hit, -jnp.inf, s)
    return vals, picks


def _candidates(t1, t2, combine, fill):
    k = PEER_TOPK
    sub = lax.broadcasted_iota(jnp.int32, (SUBLANES, t1.shape[1]), 0)
    blocks = [combine(t1[0:1], t2)]
    for a in range(1, SUBLANES):
        n = k // (a + 1)
        blk = combine(t1[a:a + 1], t2[0:SUBLANES])
        blocks.append(blk if n >= SUBLANES else jnp.where(sub < n, blk, fill))
    blocks.append(combine(t1[SUBLANES:k], t2[0:1]))
    return jnp.concatenate(blocks, axis=0)


def _topk_body(st_ref, idx_ref, gate_ref):
    idx_rows, gate_rows = [], []
    for h in range(PEER_HEADS):
        v1, i1 = _topk_rows(st_ref[(2 * h) * N_KEYS:(2 * h + 1) * N_KEYS, :], PEER_TOPK)
        v2, i2 = _topk_rows(st_ref[(2 * h + 1) * N_KEYS:(2 * h + 2) * N_KEYS, :], PEER_TOPK)
        cand_s = _candidates(v1, v2, lambda x, y: x + y, -jnp.inf)
        cand_e = _candidates(i1, i2, lambda x, y: x * N_KEYS + y, -1)
        best, expert = _topk_rows(cand_s, PEER_TOPK, payload=cand_e)
        e = jnp.exp(best - best[0:1])
        gate_rows.append(e / jnp.sum(e, axis=0, keepdims=True))
        idx_rows.append(expert * PEER_ROW_SUBLANES)
    gate_ref[...] = jnp.concatenate(gate_rows, axis=0)
    idx_ref[...] = jnp.concatenate(idx_rows, axis=0).T


def _topk(scores_t, tb):
    t = scores_t.shape[1]
    return pl.pallas_call(
        _topk_body,
        grid=(t // tb,),
        in_specs=[pl.BlockSpec((scores_t.shape[0], tb), lambda i: (0, i))],
        out_specs=[
            pl.BlockSpec((tb, PEER_SLOTS), lambda i: (i, 0)),
            pl.BlockSpec((PEER_SLOTS, tb), lambda i: (0, i)),
        ],
        out_shape=[
            jax.ShapeDtypeStruct((t, PEER_SLOTS), jnp.int32),
            jax.ShapeDtypeStruct((PEER_SLOTS, t), F32),
        ],
        compiler_params=_cparams(1),
        name="peer_topk",
    )(scores_t)


def _pack_table(tab):
    tb = lax.bitcast_convert_type(tab.astype(BF16), jnp.uint16).astype(jnp.uint32)
    words = tb[:, :PEER_ROW_WORDS] | (tb[:, PEER_ROW_WORDS:] << 16)
    return words.reshape(tab.shape[0] * PEER_ROW_SUBLANES, LANES)


PEER_TILE_ROWS = PEER_SLOTS * PEER_ROW_SUBLANES
PEER_UNROLL = SUBLANES


def _gather_rows(tab_ref, idx_ref, tt, dst_ref, u):
    rs = PEER_ROW_SUBLANES
    for k in range(PEER_SLOTS):
        off = pl.multiple_of(idx_ref[tt, k], rs)
        dst_ref[u, k * rs:(k + 1) * rs, :] = tab_ref[pl.ds(off, rs), :]


def _unpack_rows(words):
    lo = lax.bitcast_convert_type(words << 16, F32)
    hi = lax.bitcast_convert_type(words & jnp.uint32(0xFFFF0000), F32)
    return lo, hi


def _table_spec(tab):
    return pl.BlockSpec(tab.shape, lambda i: (0, 0), pipeline_mode=pl.Buffered(1))


def _peer_act_body(idx_ref, h_ref, gate_ref, g4_ref, tab_ref, coef4_ref, rows_ref, act_ref, *, tb):
    rs = PEER_ROW_SUBLANES
    n_tiles = PEER_TILE_ROWS // SUBLANES
    lane = lax.broadcasted_iota(jnp.int32, (PEER_SLOTS, tb), 1)
    g4 = g4_ref[...]
    act_ref[...] = jnp.zeros(act_ref.shape, F32)

    def group(g, carry):
        for u in range(PEER_UNROLL):
            tt = g * PEER_UNROLL + u
            _gather_rows(tab_ref, idx_ref, tt, rows_ref, u)
            base = pl.multiple_of(g * (PEER_UNROLL * SUBLANES), PEER_UNROLL * SUBLANES) + u * SUBLANES
            h_lo = h_ref[pl.ds(base, rs), :]
            h_hi = h_ref[pl.ds(base + rs, rs), :]
            h2_lo = jnp.concatenate([h_lo, h_lo], axis=0)[None]
            h2_hi = jnp.concatenate([h_hi, h_hi], axis=0)[None]
            lo, hi = _unpack_rows(rows_ref[u])
            prod = (lo.reshape(n_tiles, SUBLANES, LANES) * h2_lo
                    + hi.reshape(n_tiles, SUBLANES, LANES) * h2_hi).reshape(PEER_TILE_ROWS, LANES)
            p_hi, p_lo = _split(prod)
            part = _dot(g4, p_hi) + _dot(g4, p_lo)
            act = jnp.sum(part, axis=1, keepdims=True)
            act_ref[...] = jnp.where(lane == tt, act, act_ref[...])
        return carry

    lax.fori_loop(0, tb // PEER_UNROLL, group, 0)
    coef = (_gelu(act_ref[...]) * gate_ref[...]).T
    coef4_ref[...] = _dot_x2(coef, g4)


def _peer_act(idx, h_rows, gate_t, tab, tb):
    t = idx.shape[0]
    slot_of_row = jnp.arange(PEER_TILE_ROWS) // PEER_ROW_SUBLANES
    g4 = (slot_of_row[None, :] == jnp.arange(PEER_SLOTS)[:, None]).astype(BF16)
    return pl.pallas_call(
        functools.partial(_peer_act_body, tb=tb),
        grid=(t // tb,),
        in_specs=[
            pl.BlockSpec((tb, PEER_SLOTS), lambda i: (i, 0), memory_space=pltpu.SMEM),
            pl.BlockSpec((tb * SUBLANES, LANES), lambda i: (i, 0)),
            pl.BlockSpec((PEER_SLOTS, tb), lambda i: (0, i)),
            _full(g4.shape),
            _table_spec(tab),
        ],
        out_specs=pl.BlockSpec((tb, PEER_TILE_ROWS), lambda i: (i, 0)),
        out_shape=jax.ShapeDtypeStruct((t, PEER_TILE_ROWS), F32),
        scratch_shapes=[
            pltpu.VMEM((PEER_UNROLL, PEER_TILE_ROWS, LANES), jnp.uint32),
            pltpu.VMEM((PEER_SLOTS, tb), F32),
        ],
        compiler_params=_cparams(1),
        name="peer_act",
    )(idx, h_rows, gate_t, g4, tab)


def _peer_out_body(idx_ref, coef4_ref, x_ref, tab_ref, o_ref, rows_ref, *, tb):
    rs = PEER_ROW_SUBLANES
    sub = lax.broadcasted_iota(jnp.int32, (SUBLANES, PEER_TILE_ROWS), 0)
    col = lax.broadcasted_iota(jnp.int32, (SUBLANES, PEER_TILE_ROWS), 1)
    own_sublane = sub == col % rs

    def group(g, carry):
        row0 = pl.multiple_of(g * PEER_UNROLL, PEER_UNROLL)
        coef8 = coef4_ref[pl.ds(row0, PEER_UNROLL), :]
        for u in range(PEER_UNROLL):
            _gather_rows(tab_ref, idx_ref, row0 + u, rows_ref, u)
            lo, hi = _unpack_rows(rows_ref[u])
            c = jnp.where(own_sublane, jnp.broadcast_to(coef8[u:u + 1], own_sublane.shape), 0.0)
            c_hi = c.astype(BF16).astype(F32)
            lhs = jnp.concatenate([c_hi, c - c_hi], axis=0).astype(BF16)
            acc_lo = _dot(lhs, lo.astype(BF16))
            acc_hi = _dot(lhs, hi.astype(BF16))
            acc_lo = acc_lo[0:SUBLANES] + acc_lo[SUBLANES:]
            acc_hi = acc_hi[0:SUBLANES] + acc_hi[SUBLANES:]
            base = pl.multiple_of(g * (PEER_UNROLL * SUBLANES), PEER_UNROLL * SUBLANES) + u * SUBLANES
            o_ref[pl.ds(base, rs), :] = x_ref[pl.ds(base, rs), :] + acc_lo[0:rs]
            o_ref[pl.ds(base + rs, rs), :] = x_ref[pl.ds(base + rs, rs), :] + acc_hi[0:rs]
        return carry

    lax.fori_loop(0, tb // PEER_UNROLL, group, 0)


def _peer_out(idx, coef4, x_rows, tab, tb):
    t = idx.shape[0]
    row_spec = pl.BlockSpec((tb * SUBLANES, LANES), lambda i: (i, 0))
    return pl.pallas_call(
        functools.partial(_peer_out_body, tb=tb),
        grid=(t // tb,),
        in_specs=[
            pl.BlockSpec((tb, PEER_SLOTS), lambda i: (i, 0), memory_space=pltpu.SMEM),
            pl.BlockSpec((tb, PEER_TILE_ROWS), lambda i: (i, 0)),
            row_spec,
            _table_spec(tab),
        ],
        out_specs=row_spec,
        out_shape=jax.ShapeDtypeStruct(x_rows.shape, F32),
        scratch_shapes=[pltpu.VMEM((PEER_UNROLL, PEER_TILE_ROWS, LANES), jnp.uint32)],
        compiler_params=_cparams(1),
        name="peer_out",
    )(idx, coef4, x_rows, tab)


def _tiles(bsz, seq):
    tl = min(seq, 512)
    tq = min(seq, 256)
    tt_s5 = min(seq, 128)
    tt_rw = min(seq, 64)
    tb = 128
    return tl, tq, tt_s5, tt_rw, tb


def kernel(x, positions, norm_mix, w_in, da_q_norm, da_k_norm, da_lambda, da_subln, ssm_lambda_re, ssm_lambda_im, ssm_b_re, ssm_b_im, ssm_c_re, ssm_c_im, ssm_d, ssm_log_dt, ssm_w_glu, ssm_b_glu, ssm_out_norm, rw_mu, rw_decay_up, rw_decay_w0, rw_aaa_up, rw_aaa_a0, rw_gate_up, rw_k_k, rw_k_a, rw_r_k, rw_ln_w, rw_ln_b, w_out, norm_ffn, peer_w_q, peer_sub_keys, peer_u, peer_v):
    bsz, seq, _ = x.shape
    depth = w_in.shape[0]
    t = bsz * seq
    tl, tq, tt_s5, tt_rw, tb = _tiles(bsz, seq)
    rope = _rope_tables(positions, tl)

    for l in range(depth):
        w = w_in[l].astype(BF16)
        cols_a, cols_b, cols_c = _inproj(
            x, norm_mix[l].reshape(1, D_MODEL), w[:, :DA_COLS],
            w[:, DA_COLS:DA_COLS + SSM_WIDTH], w[:, DA_COLS + SSM_WIDTH:], tl)

        q, k, v = _attn_prep(cols_a.reshape(t, DA_COLS), rope,
                             da_q_norm[l].reshape(1, DA_WIDTH), da_k_norm[l].reshape(1, DA_WIDTH), tl)
        lam_init = 0.8 - 0.6 * math.exp(-0.3 * l)
        lam_pad = (jnp.zeros((SUBLANES, LANES), F32)
                   .at[0:4, 0:DA_HEAD_DIM].set(da_lambda[l]).at[4, :].set(lam_init))
        y_a = _attention(q.reshape(bsz, seq, DA_WIDTH), k.reshape(bsz, seq, DA_WIDTH),
                         v.reshape(bsz, seq, DA_WIDTH), lam_pad, da_subln[l].reshape(1, DA_V_DIM), tq)

        s5_ops = _s5_operands(ssm_lambda_re[l], ssm_lambda_im[l], ssm_b_re[l], ssm_b_im[l],
                              ssm_c_re[l], ssm_c_im[l], ssm_d[l], ssm_log_dt[l], ssm_w_glu[l],
                              ssm_b_glu[l], ssm_out_norm[l])
        y_b = _s5(cols_b.reshape(t, SSM_WIDTH), s5_ops, tt_s5, bsz)

        rw_ops = _rwkv_operands(rw_mu[l], rw_decay_up[l], rw_decay_w0[l], rw_aaa_up[l], rw_aaa_a0[l],
                                rw_gate_up[l], rw_k_k[l], rw_k_a[l], rw_r_k[l], rw_ln_w[l], rw_ln_b[l])
        y_c = _rwkv(cols_c.reshape(t, RW_COLS), rw_ops, tt_rw, bsz)

        wo = w_out[l].astype(BF16)
        keys = peer_sub_keys[l].reshape(2 * PEER_HEADS, N_KEYS, PEER_KEY_DIM).astype(BF16)
        x_mid, h_ffn, scores_t = _outproj(
            x, y_a, y_b.reshape(seq, bsz * SSM_WIDTH), y_c.reshape(seq, bsz * RW_WIDTH),
            wo[:DA_WIDTH], wo[DA_WIDTH:DA_WIDTH + SSM_WIDTH], wo[DA_WIDTH + SSM_WIDTH:],
            norm_ffn[l].reshape(1, D_MODEL), peer_w_q[l].astype(BF16), keys, tl)

        idx, gate_t = _topk(scores_t, tb)
        coef4 = _peer_act(idx, h_ffn.reshape(t * SUBLANES, LANES), gate_t, _pack_table(peer_u[l]), tb)
        x = _peer_out(idx, coef4, x_mid.reshape(t * SUBLANES, LANES),
                      _pack_table(peer_v[l]), tb).reshape(bsz, seq, D_MODEL)
    return x
```

```python
import functools
import math

import jax
import jax.numpy as jnp
from jax import lax
from jax.experimental import pallas as pl
from jax.experimental.pallas import tpu as pltpu

D_MODEL = 1024
CHUNK = 64
NORM_EPS = 1e-6

DA_HEADS = 4
DA_HEAD_DIM = 64
DA_V_DIM = 128
DA_WIDTH = 512
ROT_DIM = 16
ROPE_THETA = 500000.0

SSM_GROUP = 16
SSM_GROUPS = 16
SSM_WIDTH = 256
SSM_STATE = 64

RW_HEADS = 4
RW_HEAD_DIM = 64
RW_WIDTH = 256
RW_DECAY_RANK = 32
RW_AAA_RANK = 32
RW_GATE_RANK = 64
RW_LN_EPS = 64e-5
RW_COLS = 896

DA_COLS = 1536

PEER_HEADS = 8
N_KEYS = 128
PEER_KEY_DIM = 128
PEER_TOPK = 16
PEER_SLOTS = PEER_HEADS * PEER_TOPK
PEER_ROW_WORDS = D_MODEL // 2
PEER_ROW_SUBLANES = PEER_ROW_WORDS // 128

LANES = 128
SUBLANES = 8
VMEM_LIMIT_BYTES = 56 * 1024 * 1024

F32 = jnp.float32
BF16 = jnp.bfloat16


def _cparams(n_axes, vmem=VMEM_LIMIT_BYTES):
    return pltpu.CompilerParams(
        dimension_semantics=("arbitrary",) * n_axes, vmem_limit_bytes=vmem)


def _dot(a, b):
    return jnp.dot(a, b, preferred_element_type=F32)


def _split(x):
    hi = x.astype(BF16)
    lo = (x - hi.astype(F32)).astype(BF16)
    return hi, lo


def _dot_x2(x, w):
    hi, lo = _split(x)
    return _dot(hi, w) + _dot(lo, w)


def _dot_x2k(x, w_stacked):
    hi, lo = _split(x)
    return _dot(jnp.concatenate([hi, lo], axis=1), w_stacked)


def _dot_x3(x, w_hi, w_lo):
    hi, lo = _split(x)
    return _dot(hi, w_hi) + _dot(lo, w_hi) + _dot(hi, w_lo)


def _wsplit(w):
    hi = w.astype(BF16)
    lo = (w - hi.astype(F32)).astype(BF16)
    return hi, lo


def _sigmoid(x):
    return 1.0 / (1.0 + jnp.exp(-x))


def _gelu(x):
    return 0.5 * x * (1.0 + lax.erf(x * (1.0 / math.sqrt(2.0))))


def _seg_ones(n, seg):
    r = jnp.arange(n) // seg
    return (r[:, None] == r[None, :]).astype(BF16)


def _full(shape):
    nd = len(shape)
    return pl.BlockSpec(shape, lambda *_: (0,) * nd)


def _inproj_body(x_ref, g_ref, wa_ref, wb_ref, wc_ref, a_ref, b_ref, c_ref):
    x = x_ref[...]
    ms = jnp.mean(x * x, axis=-1, keepdims=True)
    h = (x * lax.rsqrt(ms + NORM_EPS) * g_ref[...]).astype(BF16)
    a_ref[...] = _dot(h, wa_ref[...])
    b_ref[...] = _dot(h, wb_ref[...])
    c_ref[...] = _dot(h, wc_ref[...])


def _inproj(x, g, wa, wb, wc, tl):
    bsz, seq, _ = x.shape
    return pl.pallas_call(
        _inproj_body,
        grid=(bsz, seq // tl),
        in_specs=[
            pl.BlockSpec((None, tl, D_MODEL), lambda b, i: (b, i, 0)),
            _full((1, D_MODEL)),
            _full((D_MODEL, DA_COLS)),
            _full((D_MODEL, SSM_WIDTH)),
            _full((D_MODEL, RW_COLS)),
        ],
        out_specs=[
            pl.BlockSpec((None, tl, DA_COLS), lambda b, i: (b, i, 0)),
            pl.BlockSpec((tl, SSM_WIDTH), lambda b, i: (i, b)),
            pl.BlockSpec((tl, RW_COLS), lambda b, i: (i, b)),
        ],
        out_shape=[
            jax.ShapeDtypeStruct((bsz, seq, DA_COLS), F32),
            jax.ShapeDtypeStruct((seq, bsz * SSM_WIDTH), F32),
            jax.ShapeDtypeStruct((seq, bsz * RW_COLS), F32),
        ],
        compiler_params=_cparams(2),
        name="inproj",
    )(x, g, wa, wb, wc)


def _rope_body(pos_ref, invf_ref, c_ref, s1_ref, s2_ref):
    ang = pos_ref[...].astype(F32) * invf_ref[...]
    c, s = jnp.cos(ang), jnp.sin(ang)
    lm = lax.broadcasted_iota(jnp.int32, ang.shape, 1) % DA_HEAD_DIM
    half = ROT_DIM // 2
    c_ref[...] = jnp.where(lm < ROT_DIM, c, 1.0)
    s1_ref[...] = jnp.where(lm < half, -s, 0.0)
    s2_ref[...] = jnp.where((lm >= half) & (lm < ROT_DIM), s, 0.0)


def _rope_tables(positions, tm):
    t = positions.size
    half = ROT_DIM // 2
    inv_freq = ROPE_THETA ** (-jnp.arange(0, ROT_DIM, 2, dtype=F32) / ROT_DIM)
    invf = inv_freq[jnp.arange(LANES) % half][None, :]
    pos = jnp.broadcast_to(positions.reshape(t, 1), (t, LANES))
    spec = pl.BlockSpec((tm, LANES), lambda i: (i, 0))
    return pl.pallas_call(
        _rope_body,
        grid=(t // tm,),
        in_specs=[spec, _full((1, LANES))],
        out_specs=[spec, spec, spec],
        out_shape=[jax.ShapeDtypeStruct((t, LANES), F32)] * 3,
        compiler_params=_cparams(1),
        name="rope_tables",
    )(pos, invf)


def _attn_prep_body(a_ref, c_ref, s1_ref, s2_ref, qg_ref, kg_ref, seg_ref,
                    q_ref, k_ref, v_ref):
    reps = DA_WIDTH // LANES
    c = jnp.tile(c_ref[...], (1, reps))
    s1 = jnp.tile(s1_ref[...], (1, reps))
    s2 = jnp.tile(s2_ref[...], (1, reps))
    seg = seg_ref[...]
    half = ROT_DIM // 2

    def norm_rope(t, gain):
        ms = _dot_x2(t * t, seg) * (1.0 / DA_HEAD_DIM)
        tn = t * lax.rsqrt(ms + NORM_EPS) * gain
        up = pltpu.roll(tn, DA_WIDTH - half, axis=1)
        dn = pltpu.roll(tn, half, axis=1)
        return tn * c + up * s1 + dn * s2

    q = norm_rope(a_ref[:, 0:DA_WIDTH], qg_ref[...])
    k = norm_rope(a_ref[:, DA_WIDTH:2 * DA_WIDTH], kg_ref[...])
    q_ref[...] = (q * (DA_HEAD_DIM ** -0.5)).astype(BF16)
    k_ref[...] = k.astype(BF16)
    v_ref[...] = a_ref[:, 2 * DA_WIDTH:3 * DA_WIDTH].astype(BF16)


def _attn_prep(cols_a, rope, qg, kg, tm):
    t = cols_a.shape[0]
    c, s1, s2 = rope
    lane_spec = pl.BlockSpec((tm, LANES), lambda i: (i, 0))
    out_spec = pl.BlockSpec((tm, DA_WIDTH), lambda i: (i, 0))
    return pl.pallas_call(
        _attn_prep_body,
        grid=(t // tm,),
        in_specs=[
            pl.BlockSpec((tm, DA_COLS), lambda i: (i, 0)),
            lane_spec, lane_spec, lane_spec,
            _full((1, DA_WIDTH)), _full((1, DA_WIDTH)),
            _full((DA_WIDTH, DA_WIDTH)),
        ],
        out_specs=[out_spec, out_spec, out_spec],
        out_shape=[jax.ShapeDtypeStruct((t, DA_WIDTH), BF16)] * 3,
        compiler_params=_cparams(1),
        name="attn_prep",
    )(cols_a, c, s1, s2, qg, kg, _seg_ones(DA_WIDTH, DA_HEAD_DIM))


def _attn_body(q_ref, k_ref, v_ref, lam_ref, sub_ref, o_ref, m_ref, l_ref, acc_ref, s_ref, p_ref,
               *, tq, tw):
    qi = pl.program_id(2)
    q = q_ref[...]
    lane = lax.broadcasted_iota(jnp.int32, q.shape, 1)
    zero = jnp.zeros_like(q)
    qmaps = (jnp.where(lane < DA_HEAD_DIM, q, zero), jnp.where(lane >= DA_HEAD_DIM, q, zero))

    m_ref[...] = jnp.full(m_ref.shape, -jnp.inf, F32)
    l_ref[...] = jnp.zeros(l_ref.shape, F32)
    acc_ref[...] = jnp.zeros(acc_ref.shape, F32)

    def span(c0, masked):
        c0 = pl.multiple_of(c0, tw)
        k = k_ref[pl.ds(c0, tw), :]
        v = v_ref[pl.ds(c0, tw), :]
        if masked:
            row_chunk = (lax.broadcasted_iota(jnp.int32, (tq, tw), 0) + tq) // CHUNK
            col_chunk = (lax.broadcasted_iota(jnp.int32, (tq, tw), 1) + (c0 - qi * tq + tq)) // CHUNK
            visible = col_chunk <= row_chunk
        for m in range(2):
            s = lax.dot_general(qmaps[m], k, (((1,), (1,)), ((), ())),
                                preferred_element_type=F32)
            s_ref[m] = jnp.where(visible, s, -jnp.inf) if masked else s
        for m in range(2):
            s = s_ref[m]
            m_old = m_ref[m]
            m_new = jnp.maximum(m_old, jnp.max(s, axis=1, keepdims=True))
            alpha = jnp.exp(m_old - m_new)
            p = jnp.exp(s - jnp.tile(m_new, (1, tw // LANES)))
            p_ref[m] = p.astype(BF16)
            l_ref[m] = alpha * l_ref[m] + jnp.sum(p, axis=1, keepdims=True)
            acc_ref[m] = alpha * acc_ref[m]
            m_ref[m] = m_new
        for m in range(2):
            acc_ref[m] += _dot(p_ref[m], v)

    def full_span(c, carry):
        span(c * tw, False)
        return carry

    n_full = qi // (tw // tq)
    lax.fori_loop(0, n_full, full_span, 0)
    span(n_full * tw, True)

    lam = lam_ref[...]
    lam_full = (jnp.exp(jnp.sum(lam[0:1] * lam[1:2], axis=1, keepdims=True))
                - jnp.exp(jnp.sum(lam[2:3] * lam[3:4], axis=1, keepdims=True))
                + lam[4:5, 0:1])
    lam_init = lam[4:5, 0:1]
    o = acc_ref[0] / l_ref[0] - lam_full * (acc_ref[1] / l_ref[1])
    ms = jnp.mean(o * o, axis=-1, keepdims=True)
    o_ref[...] = o * lax.rsqrt(ms + NORM_EPS) * sub_ref[...] * (1.0 - lam_init)


def _attention(q, k, v, lam_pad, subln, tq):
    bsz, seq, _ = q.shape
    tw = min(2 * tq, seq)
    kv_spec = pl.BlockSpec((None, seq, DA_V_DIM), lambda b, h, i: (b, 0, h))
    return pl.pallas_call(
        functools.partial(_attn_body, tq=tq, tw=tw),
        grid=(bsz, DA_HEADS, seq // tq),
        in_specs=[
            pl.BlockSpec((None, tq, DA_V_DIM), lambda b, h, i: (b, i, h)),
            kv_spec, kv_spec,
            _full((SUBLANES, LANES)),
            _full((1, DA_V_DIM)),
        ],
        out_specs=pl.BlockSpec((None, tq, DA_V_DIM), lambda b, h, i: (b, i, h)),
        out_shape=jax.ShapeDtypeStruct((bsz, seq, DA_WIDTH), F32),
        scratch_shapes=[
            pltpu.VMEM((2, tq, LANES), F32),
            pltpu.VMEM((2, tq, LANES), F32),
            pltpu.VMEM((2, tq, DA_V_DIM), F32),
            pltpu.VMEM((2, tq, tw), F32),
            pltpu.VMEM((2, tq, tw), BF16),
        ],
        compiler_params=_cparams(3),
        name="diff_attention",
    )(q, k, v, lam_pad, subln)


def _s5_body(u_ref, bhi_ref, blo_ref, are_ref, aim_ref, chi_ref, clo_ref, d_ref,
             whi_ref, wlo_ref, bg_ref, g_ref, o_ref, st_ref, x_ref, *, tt, bsz):
    ns = SSM_GROUPS * SSM_STATE

    @pl.when(pl.program_id(0) == 0)
    def _():
        st_ref[...] = jnp.zeros(st_ref.shape, F32)

    u = u_ref[...]
    x_ref[...] = _dot_x3(u, bhi_ref[...], blo_ref[...])
    a_re = jnp.broadcast_to(are_ref[...], (bsz, ns))
    a_im = jnp.broadcast_to(aim_ref[...], (bsz, ns))

    def step(t, carry):
        xr, xi = carry
        r = pl.multiple_of(t * bsz, bsz)
        nr = a_re * xr - a_im * xi + x_ref[pl.ds(r, bsz), 0:ns]
        ni = a_re * xi + a_im * xr + x_ref[pl.ds(r, bsz), ns:2 * ns]
        x_ref[pl.ds(r, bsz), 0:ns] = nr
        x_ref[pl.ds(r, bsz), ns:2 * ns] = ni
        return nr, ni

    xr, xi = lax.fori_loop(0, tt, step, (st_ref[:, 0:ns], st_ref[:, ns:2 * ns]))
    st_ref[:, 0:ns] = xr
    st_ref[:, ns:2 * ns] = xi

    y = _dot_x3(x_ref[...], chi_ref[...], clo_ref[...]) + d_ref[...] * u
    yg = _gelu(y)
    z = _dot_x3(yg, whi_ref[...], wlo_ref[...]) + bg_ref[...]
    out = yg * _sigmoid(z)
    ms = jnp.mean(out * out, axis=-1, keepdims=True)
    o_ref[...] = out * lax.rsqrt(ms + NORM_EPS) * g_ref[...]


def _s5(u, ops, tt, bsz):
    rows = u.shape[0]
    ns2 = 2 * SSM_GROUPS * SSM_STATE
    blk = tt * bsz
    io_spec = pl.BlockSpec((blk, SSM_WIDTH), lambda i: (i, 0))
    return pl.pallas_call(
        functools.partial(_s5_body, tt=tt, bsz=bsz),
        grid=(rows // blk,),
        in_specs=[
            io_spec,
            _full((SSM_WIDTH, ns2)), _full((SSM_WIDTH, ns2)),
            _full((1, ns2 // 2)), _full((1, ns2 // 2)),
            _full((ns2, SSM_WIDTH)), _full((ns2, SSM_WIDTH)),
            _full((1, SSM_WIDTH)),
            _full((SSM_WIDTH, SSM_WIDTH)), _full((SSM_WIDTH, SSM_WIDTH)),
            _full((1, SSM_WIDTH)), _full((1, SSM_WIDTH)),
        ],
        out_specs=io_spec,
        out_shape=jax.ShapeDtypeStruct((rows, SSM_WIDTH), F32),
        scratch_shapes=[
            pltpu.VMEM((bsz, ns2), F32),
            pltpu.VMEM((blk, ns2), F32),
        ],
        compiler_params=_cparams(1),
        name="s5",
    )(u, *ops)


def _s5_operands(lam_re, lam_im, b_re, b_im, c_re, c_im, d_skip, log_dt, w_glu, b_glu, out_norm):
    lr = jnp.minimum(lam_re, -1e-4)
    li = lam_im
    step = jnp.exp(log_dt)[:, None]
    mag = jnp.exp(lr * step)
    abar_re = mag * jnp.cos(li * step)
    abar_im = mag * jnp.sin(li * step)
    den = lr * lr + li * li
    coef_re = ((abar_re - 1.0) * lr + abar_im * li) / den
    coef_im = (abar_im * lr - (abar_re - 1.0) * li) / den
    bbar_re = coef_re[..., None] * b_re - coef_im[..., None] * b_im
    bbar_im = coef_re[..., None] * b_im + coef_im[..., None] * b_re
    eye = jnp.eye(SSM_GROUPS, dtype=F32)
    ns = SSM_GROUPS * SSM_STATE

    def in_mat(bb):
        return jnp.einsum('gph,gk->ghkp', bb, eye).reshape(SSM_WIDTH, ns)

    def out_mat(cc):
        return jnp.einsum('ghp,gk->kpgh', cc, eye).reshape(ns, SSM_WIDTH)

    bmat = jnp.concatenate([in_mat(bbar_re), in_mat(bbar_im)], axis=1)
    cmat = jnp.concatenate([out_mat(c_re), -out_mat(c_im)], axis=0)
    return (*_wsplit(bmat), abar_re.reshape(1, ns), abar_im.reshape(1, ns), *_wsplit(cmat),
            d_skip.reshape(1, SSM_WIDTH), *_wsplit(w_glu), b_glu.reshape(1, SSM_WIDTH),
            out_norm.reshape(1, SSM_WIDTH))


def _rwkv_body(c_ref, mu_ref, wdh_ref, wdl_ref, wah_ref, wal_ref, wgh_ref, wgl_ref,
               w0_ref, a0_ref, kk_ref, ka_ref, rk_ref, lnw_ref, lnb_ref,
               seg256_ref, seg128_ref, eye_ref,
               o_ref, prev_ref, st_ref, r_s, w_s, k_s, v_s, a_s, b_s, y_s, *, tt, bsz):
    n_pairs = RW_WIDTH // LANES
    n_inst = n_pairs * bsz
    hd = RW_HEAD_DIM

    @pl.when(pl.program_id(0) == 0)
    def _():
        prev_ref[...] = jnp.zeros(prev_ref.shape, F32)
        st_ref[...] = jnp.zeros(st_ref.shape, F32)

    c = c_ref[...]
    rows = c.shape[0]
    prev = jnp.concatenate([prev_ref[...], c[0:rows - bsz]], axis=0)
    prev_ref[...] = c[rows - bsz:rows]
    cs = c + (prev - c) * mu_ref[...]
    r = cs[:, 0:RW_WIDTH]
    k = cs[:, RW_WIDTH:2 * RW_WIDTH]
    v = cs[:, 2 * RW_WIDTH:3 * RW_WIDTH]
    x3 = cs[:, 3 * RW_WIDTH:RW_COLS]

    seg256 = seg256_ref[...]
    wdec = w0_ref[...] + _dot_x3(jnp.tanh(x3), wdh_ref[...], wdl_ref[...])
    z = -wdec
    softplus = jnp.maximum(z, 0.0) + jnp.log(1.0 + jnp.exp(-jnp.abs(z)))
    decay = jnp.exp(-jnp.exp(-softplus - 0.5))
    a = _sigmoid(a0_ref[...] + _dot_x3(x3, wah_ref[...], wal_ref[...]))
    g = _dot_x3(_sigmoid(x3), wgh_ref[...], wgl_ref[...])
    kk = k * kk_ref[...]
    kk = kk / jnp.maximum(jnp.sqrt(_dot_x2(kk * kk, seg256)), 1e-12)
    k2 = k * (1.0 + (a - 1.0) * ka_ref[...])
    r_s[...] = r
    w_s[...] = decay
    k_s[...] = k2
    v_s[...] = v
    a_s[...] = -kk
    b_s[...] = kk * a

    seg128x2 = seg128_ref[...]
    seg128 = seg128x2[0:LANES]
    eye = eye_ref[...]
    sub = lax.broadcasted_iota(jnp.int32, (bsz, LANES), 0)

    def step(t, carry):
        row0 = pl.multiple_of(t * bsz, bsz)
        blk = [ref[pl.ds(row0, bsz), :] for ref in (r_s, w_s, k_s, v_s, a_s, b_s)]

        def vec(q, inst):
            p, b = divmod(inst, bsz)
            return blk[q][b:b + 1, p * LANES:(p + 1) * LANES]

        states = [st_ref[i * hd:(i + 1) * hd, :] for i in range(n_inst)]
        lhs_a = jnp.concatenate([states[i] * vec(4, i) for i in range(n_inst)], axis=0)
        lhs_v = jnp.concatenate([eye * vec(3, i) for i in range(n_inst)], axis=0)
        sa = _dot_x2k(lhs_a, seg128x2)
        vcol = _dot_x2k(lhs_v, seg128x2)
        new = []
        for i in range(n_inst):
            sl = slice(i * hd, (i + 1) * hd)
            new.append(states[i] * vec(1, i) + sa[sl] * vec(5, i) + vcol[sl] * vec(2, i))
        lhs_y = jnp.concatenate([new[i] * vec(0, i) for i in range(n_inst)], axis=0)
        ycol = _dot(lhs_y.astype(BF16), seg128)
        for i in range(n_inst):
            st_ref[i * hd:(i + 1) * hd, :] = new[i]
        for p in range(n_pairs):
            yblk = jnp.zeros((bsz, LANES), F32)
            for b in range(bsz):
                i = p * bsz + b
                yrow = jnp.sum(eye * ycol[i * hd:(i + 1) * hd], axis=0, keepdims=True)
                yblk = jnp.where(sub == b, yrow, yblk)
            y_s[pl.ds(row0, bsz), p * LANES:(p + 1) * LANES] = yblk
        return carry

    lax.fori_loop(0, tt, step, 0)

    y = y_s[...]
    inv = 1.0 / hd
    mean = _dot_x2(y, seg256) * inv
    dv = y - mean
    var = _dot_x2(dv * dv, seg256) * inv
    yn = dv * lax.rsqrt(var + RW_LN_EPS) * lnw_ref[...] + lnb_ref[...]
    bonus = _dot_x2(r * k2 * rk_ref[...], seg256) * v
    o_ref[...] = (yn + bonus) * g


def _rwkv(c, ops, tt, bsz):
    rows = c.shape[0]
    blk = tt * bsz
    vec_spec = _full((1, RW_WIDTH))
    pad_spec = _full((LANES, RW_WIDTH))
    n_inst = (RW_WIDTH // LANES) * bsz
    return pl.pallas_call(
        functools.partial(_rwkv_body, tt=tt, bsz=bsz),
        grid=(rows // blk,),
        in_specs=[
            pl.BlockSpec((blk, RW_COLS), lambda i: (i, 0)),
            _full((1, RW_COLS)),
            pad_spec, pad_spec, pad_spec, pad_spec, pad_spec, pad_spec,
            vec_spec, vec_spec, vec_spec, vec_spec, vec_spec, vec_spec, vec_spec,
            _full((RW_WIDTH, RW_WIDTH)), _full((2 * LANES, LANES)), _full((RW_HEAD_DIM, LANES)),
        ],
        out_specs=pl.BlockSpec((blk, RW_WIDTH), lambda i: (i, 0)),
        out_shape=jax.ShapeDtypeStruct((rows, RW_WIDTH), F32),
        scratch_shapes=[
            pltpu.VMEM((bsz, RW_COLS), F32),
            pltpu.VMEM((n_inst * RW_HEAD_DIM, LANES), F32),
        ] + [pltpu.VMEM((blk, RW_WIDTH), F32)] * 7,
        compiler_params=_cparams(1),
        name="rwkv7",
    )(c, *ops)


def _rwkv_operands(mu, decay_up, decay_w0, aaa_up, aaa_a0, gate_up, k_k, k_a, r_k, ln_w, ln_b):
    def pad(w, off):
        return _wsplit(jnp.zeros((LANES, RW_WIDTH), F32).at[off:off + w.shape[0]].set(w))

    row = lambda t: t.reshape(1, RW_WIDTH)
    lane = jnp.arange(LANES)
    eye = (lane[None, :] % RW_HEAD_DIM == jnp.arange(RW_HEAD_DIM)[:, None]).astype(F32)
    return (mu.reshape(1, RW_COLS),
            *pad(decay_up, 0), *pad(aaa_up, RW_DECAY_RANK),
            *pad(gate_up, RW_DECAY_RANK + RW_AAA_RANK),
            row(decay_w0), row(aaa_a0), row(k_k), row(k_a), row(r_k), row(ln_w), row(ln_b),
            _seg_ones(RW_WIDTH, RW_HEAD_DIM),
            jnp.concatenate([_seg_ones(LANES, RW_HEAD_DIM)] * 2, axis=0), eye)


def _outproj_body(x_ref, ya_ref, yb_ref, yc_ref, wa_ref, wb_ref, wc_ref, g_ref, wq_ref,
                  keys_ref, xn_ref, h_ref, st_ref):
    xn = (x_ref[...]
          + _dot(ya_ref[...].astype(BF16), wa_ref[...])
          + _dot(yb_ref[...].astype(BF16), wb_ref[...])
          + _dot(yc_ref[...].astype(BF16), wc_ref[...]))
    xn_ref[...] = xn
    ms = jnp.mean(xn * xn, axis=-1, keepdims=True)
    h = xn * lax.rsqrt(ms + NORM_EPS) * g_ref[...]
    h_ref[...] = h
    q = _dot(h.astype(BF16), wq_ref[...]).astype(BF16)
    for hm in range(2 * PEER_HEADS):
        st_ref[hm * N_KEYS:(hm + 1) * N_KEYS, :] = lax.dot_general(
            keys_ref[hm], q[:, hm * PEER_KEY_DIM:(hm + 1) * PEER_KEY_DIM],
            (((1,), (1,)), ((), ())), preferred_element_type=F32)


def _outproj(x, ya, yb, yc, wa, wb, wc, g, wq, keys, tl):
    bsz, seq, _ = x.shape
    nl = seq // tl
    n_rows = 2 * PEER_HEADS * N_KEYS
    tok_spec = pl.BlockSpec((None, tl, D_MODEL), lambda b, i: (b, i, 0))
    tm_spec = pl.BlockSpec((tl, SSM_WIDTH), lambda b, i: (i, b))
    return pl.pallas_call(
        _outproj_body,
        grid=(bsz, nl),
        in_specs=[
            tok_spec,
            pl.BlockSpec((None, tl, DA_WIDTH), lambda b, i: (b, i, 0)),
            tm_spec, tm_spec,
            _full((DA_WIDTH, D_MODEL)), _full((SSM_WIDTH, D_MODEL)), _full((RW_WIDTH, D_MODEL)),
            _full((1, D_MODEL)),
            _full((D_MODEL, n_rows)),
            _full((2 * PEER_HEADS, N_KEYS, PEER_KEY_DIM)),
        ],
        out_specs=[
            tok_spec, tok_spec,
            pl.BlockSpec((n_rows, tl), lambda b, i: (0, b * nl + i)),
        ],
        out_shape=[
            jax.ShapeDtypeStruct((bsz, seq, D_MODEL), F32),
            jax.ShapeDtypeStruct((bsz, seq, D_MODEL), F32),
            jax.ShapeDtypeStruct((n_rows, bsz * seq), F32),
        ],
        compiler_params=_cparams(2),
        name="outproj_peer_scores",
    )(x, ya, yb, yc, wa, wb, wc, g, wq, keys)


def _topk_rows(s, k, payload=None):
    n, width = s.shape
    rows = lax.broadcasted_iota(jnp.int32, s.shape, 0)
    out_rows = lax.broadcasted_iota(jnp.int32, (k, width), 0)
    vals = jnp.zeros((k, width), F32)
    picks = jnp.zeros((k, width), jnp.int32)
    for j in range(k):
        m = jnp.max(s, axis=0, keepdims=True)
        pos = jnp.min(jnp.where(s == m, rows, n), axis=0, keepdims=True)
        hit = rows == pos
        pick = pos if payload is None else jnp.max(jnp.where(hit, payload, -1), axis=0, keepdims=True)
        vals = jnp.where(out_rows == j, m, vals)
        picks = jnp.where(out_rows == j, pick, picks)
        s = jnp.where(hit, -jnp.inf, s)
    return vals, picks


def _candidates(t1, t2, combine, fill):
    k = PEER_TOPK
    sub = lax.broadcasted_iota(jnp.int32, (SUBLANES, t1.shape[1]), 0)
    blocks = [combine(t1[0:1], t2)]
    for a in range(1, SUBLANES):
        n = k // (a + 1)
        blk = combine(t1[a:a + 1], t2[0:SUBLANES])
        blocks.append(blk if n >= SUBLANES else jnp.where(sub < n, blk, fill))
    blocks.append(combine(t1[SUBLANES:k], t2[0:1]))
    return jnp.concatenate(blocks, axis=0)


def _topk_body(st_ref, idx_ref, gate_ref):
    idx_rows, gate_rows = [], []
    for h in range(PEER_HEADS):
        v1, i1 = _topk_rows(st_ref[(2 * h) * N_KEYS:(2 * h + 1) * N_KEYS, :], PEER_TOPK)
        v2, i2 = _topk_rows(st_ref[(2 * h + 1) * N_KEYS:(2 * h + 2) * N_KEYS, :], PEER_TOPK)
        cand_s = _candidates(v1, v2, lambda x, y: x + y, -jnp.inf)
        cand_e = _candidates(i1, i2, lambda x, y: x * N_KEYS + y, -1)
        best, expert = _topk_rows(cand_s, PEER_TOPK, payload=cand_e)
        e = jnp.exp(best - best[0:1])
        gate_rows.append(e / jnp.sum(e, axis=0, keepdims=True))
        idx_rows.append(expert * PEER_ROW_SUBLANES)
    gate_ref[...] = jnp.concatenate(gate_rows, axis=0)
    idx_ref[...] = jnp.concatenate(idx_rows, axis=0).T


def _topk(scores_t, tb):
    t = scores_t.shape[1]
    return pl.pallas_call(
        _topk_body,
        grid=(t // tb,),
        in_specs=[pl.BlockSpec((scores_t.shape[0], tb), lambda i: (0, i))],
        out_specs=[
            pl.BlockSpec((tb, PEER_SLOTS), lambda i: (i, 0)),
            pl.BlockSpec((PEER_SLOTS, tb), lambda i: (0, i)),
        ],
        out_shape=[
            jax.ShapeDtypeStruct((t, PEER_SLOTS), jnp.int32),
            jax.ShapeDtypeStruct((PEER_SLOTS, t), F32),
        ],
        compiler_params=_cparams(1),
        name="peer_topk",
    )(scores_t)


def _pack_table(tab):
    tb = lax.bitcast_convert_type(tab.astype(BF16), jnp.uint16).astype(jnp.uint32)
    words = tb[:, :PEER_ROW_WORDS] | (tb[:, PEER_ROW_WORDS:] << 16)
    return words.reshape(tab.shape[0] * PEER_ROW_SUBLANES, LANES)


PEER_TILE_ROWS = PEER_SLOTS * PEER_ROW_SUBLANES
PEER_UNROLL = SUBLANES


def _gather_rows(tab_ref, idx_ref, tt, dst_ref, u):
    rs = PEER_ROW_SUBLANES
    for k in range(PEER_SLOTS):
        off = pl.multiple_of(idx_ref[tt, k], rs)
        dst_ref[pl.ds(u * PEER_TILE_ROWS + k * rs, rs), :] = tab_ref[pl.ds(off, rs), :]


def _row_plane(rows_ref, u, r):
    words = rows_ref[pl.ds(u * PEER_TILE_ROWS + r, PEER_SLOTS, stride=PEER_ROW_SUBLANES), :]
    lo = lax.bitcast_convert_type(words << 16, F32)
    hi = lax.bitcast_convert_type(words & jnp.uint32(0xFFFF0000), F32)
    return lo, hi


def _table_spec(tab):
    return pl.BlockSpec(tab.shape, lambda i: (0, 0), pipeline_mode=pl.Buffered(1))


def _peer_act_body(idx_ref, h_ref, gate_ref, tab_ref, coef_ref, rows_ref, act_ref, *, tb):
    lane = lax.broadcasted_iota(jnp.int32, (PEER_SLOTS, tb), 1)
    half = D_MODEL // 2
    act_ref[...] = jnp.zeros(act_ref.shape, F32)

    def group(g, carry):
        row0 = pl.multiple_of(g * PEER_UNROLL, PEER_UNROLL)
        h8 = h_ref[pl.ds(row0, PEER_UNROLL), :]
        for u in range(PEER_UNROLL):
            _gather_rows(tab_ref, idx_ref, row0 + u, rows_ref, u)
            part = jnp.zeros((PEER_SLOTS, LANES), F32)
            for r in range(PEER_ROW_SUBLANES):
                lo, hi = _row_plane(rows_ref, u, r)
                part = (part + lo * h8[u:u + 1, r * LANES:(r + 1) * LANES]
                        + hi * h8[u:u + 1, half + r * LANES:half + (r + 1) * LANES])
            act = jnp.sum(part, axis=1, keepdims=True)
            act_ref[...] = jnp.where(lane == row0 + u, act, act_ref[...])
        return carry

    lax.fori_loop(0, tb // PEER_UNROLL, group, 0)
    coef_ref[...] = _gelu(act_ref[...]) * gate_ref[...]


def _peer_act(idx, h, gate_t, tab, tb):
    t = idx.shape[0]
    slot_major = pl.BlockSpec((PEER_SLOTS, tb), lambda i: (0, i))
    return pl.pallas_call(
        functools.partial(_peer_act_body, tb=tb),
        grid=(t // tb,),
        in_specs=[
            pl.BlockSpec((tb, PEER_SLOTS), lambda i: (i, 0), memory_space=pltpu.SMEM),
            pl.BlockSpec((tb, D_MODEL), lambda i: (i, 0)),
            slot_major,
            _table_spec(tab),
        ],
        out_specs=slot_major,
        out_shape=jax.ShapeDtypeStruct((PEER_SLOTS, t), F32),
        scratch_shapes=[
            pltpu.VMEM((PEER_UNROLL * PEER_TILE_ROWS, LANES), jnp.uint32),
            pltpu.VMEM((PEER_SLOTS, tb), F32),
        ],
        compiler_params=_cparams(1),
        name="peer_act",
    )(idx, h, gate_t, tab)


def _peer_out_body(idx_ref, coef_ref, x_ref, ones_ref, tab_ref, o_ref, rows_ref, *, tb):
    lane = lax.broadcasted_iota(jnp.int32, (PEER_SLOTS, tb), 1)
    sub = lax.broadcasted_iota(jnp.int32, (PEER_UNROLL, LANES), 0)
    ones = ones_ref[...]
    n_planes = PEER_ROW_SUBLANES

    def group(g, carry):
        row0 = pl.multiple_of(g * PEER_UNROLL, PEER_UNROLL)
        planes = [jnp.zeros((PEER_UNROLL, LANES), F32) for _ in range(2 * n_planes)]
        for u in range(PEER_UNROLL):
            _gather_rows(tab_ref, idx_ref, row0 + u, rows_ref, u)
            ccol = _dot_x2k(jnp.where(lane == row0 + u, coef_ref[...], 0.0), ones)
            for r in range(n_planes):
                lo, hi = _row_plane(rows_ref, u, r)
                planes[r] = jnp.where(sub == u, jnp.sum(lo * ccol, axis=0, keepdims=True), planes[r])
                planes[n_planes + r] = jnp.where(
                    sub == u, jnp.sum(hi * ccol, axis=0, keepdims=True), planes[n_planes + r])
        o_ref[pl.ds(row0, PEER_UNROLL), :] = (x_ref[pl.ds(row0, PEER_UNROLL), :]
                                              + jnp.concatenate(planes, axis=1))
        return carry

    lax.fori_loop(0, tb // PEER_UNROLL, group, 0)


def _peer_out(idx, coef_t, x, tab, tb):
    t = idx.shape[0]
    tok_spec = pl.BlockSpec((tb, D_MODEL), lambda i: (i, 0))
    ones = jnp.ones((2 * tb, LANES), BF16)
    return pl.pallas_call(
        functools.partial(_peer_out_body, tb=tb),
        grid=(t // tb,),
        in_specs=[
            pl.BlockSpec((tb, PEER_SLOTS), lambda i: (i, 0), memory_space=pltpu.SMEM),
            pl.BlockSpec((PEER_SLOTS, tb), lambda i: (0, i)),
            tok_spec,
            _full(ones.shape),
            _table_spec(tab),
        ],
        out_specs=tok_spec,
        out_shape=jax.ShapeDtypeStruct(x.shape, F32),
        scratch_shapes=[pltpu.VMEM((PEER_UNROLL * PEER_TILE_ROWS, LANES), jnp.uint32)],
        compiler_params=_cparams(1),
        name="peer_out",
    )(idx, coef_t, x, ones, tab)


def _tiles(bsz, seq):
    tl = min(seq, 512)
    tq = min(seq, 256)
    tt_s5 = min(seq, 128)
    tt_rw = min(seq, 64)
    tb = 128
    return tl, tq, tt_s5, tt_rw, tb


def kernel(x, positions, norm_mix, w_in, da_q_norm, da_k_norm, da_lambda, da_subln, ssm_lambda_re, ssm_lambda_im, ssm_b_re, ssm_b_im, ssm_c_re, ssm_c_im, ssm_d, ssm_log_dt, ssm_w_glu, ssm_b_glu, ssm_out_norm, rw_mu, rw_decay_up, rw_decay_w0, rw_aaa_up, rw_aaa_a0, rw_gate_up, rw_k_k, rw_k_a, rw_r_k, rw_ln_w, rw_ln_b, w_out, norm_ffn, peer_w_q, peer_sub_keys, peer_u, peer_v):
    bsz, seq, _ = x.shape
    depth = w_in.shape[0]
    t = bsz * seq
    tl, tq, tt_s5, tt_rw, tb = _tiles(bsz, seq)
    rope = _rope_tables(positions, tl)

    for l in range(depth):
        w = w_in[l].astype(BF16)
        cols_a, cols_b, cols_c = _inproj(
            x, norm_mix[l].reshape(1, D_MODEL), w[:, :DA_COLS],
            w[:, DA_COLS:DA_COLS + SSM_WIDTH], w[:, DA_COLS + SSM_WIDTH:], tl)

        q, k, v = _attn_prep(cols_a.reshape(t, DA_COLS), rope,
                             da_q_norm[l].reshape(1, DA_WIDTH), da_k_norm[l].reshape(1, DA_WIDTH), tl)
        lam_init = 0.8 - 0.6 * math.exp(-0.3 * l)
        lam_pad = (jnp.zeros((SUBLANES, LANES), F32)
                   .at[0:4, 0:DA_HEAD_DIM].set(da_lambda[l]).at[4, :].set(lam_init))
        y_a = _attention(q.reshape(bsz, seq, DA_WIDTH), k.reshape(bsz, seq, DA_WIDTH),
                         v.reshape(bsz, seq, DA_WIDTH), lam_pad, da_subln[l].reshape(1, DA_V_DIM), tq)

        s5_ops = _s5_operands(ssm_lambda_re[l], ssm_lambda_im[l], ssm_b_re[l], ssm_b_im[l],
                              ssm_c_re[l], ssm_c_im[l], ssm_d[l], ssm_log_dt[l], ssm_w_glu[l],
                              ssm_b_glu[l], ssm_out_norm[l])
        y_b = _s5(cols_b.reshape(t, SSM_WIDTH), s5_ops, tt_s5, bsz)

        rw_ops = _rwkv_operands(rw_mu[l], rw_decay_up[l], rw_decay_w0[l], rw_aaa_up[l], rw_aaa_a0[l],
                                rw_gate_up[l], rw_k_k[l], rw_k_a[l], rw_r_k[l], rw_ln_w[l], rw_ln_b[l])
        y_c = _rwkv(cols_c.reshape(t, RW_COLS), rw_ops, tt_rw, bsz)

        wo = w_out[l].astype(BF16)
        keys = peer_sub_keys[l].reshape(2 * PEER_HEADS, N_KEYS, PEER_KEY_DIM).astype(BF16)
        x_mid, h_ffn, scores_t = _outproj(
            x, y_a, y_b.reshape(seq, bsz * SSM_WIDTH), y_c.reshape(seq, bsz * RW_WIDTH),
            wo[:DA_WIDTH], wo[DA_WIDTH:DA_WIDTH + SSM_WIDTH], wo[DA_WIDTH + SSM_WIDTH:],
            norm_ffn[l].reshape(1, D_MODEL), peer_w_q[l].astype(BF16), keys, tl)

        idx, gate_t = _topk(scores_t, tb)
        coef_t = _peer_act(idx, h_ffn.reshape(t, D_MODEL), gate_t, _pack_table(peer_u[l]), tb)
        x = _peer_out(idx, coef_t, x_mid.reshape(t, D_MODEL),
                      _pack_table(peer_v[l]), tb).reshape(bsz, seq, D_MODEL)
    return x
```

```python
import functools
import math

import jax
import jax.numpy as jnp
from jax import lax
from jax.experimental import pallas as pl
from jax.experimental.pallas import tpu as pltpu

D_MODEL = 1024
CHUNK = 64
NORM_EPS = 1e-6

DA_HEADS = 4
DA_HEAD_DIM = 64
DA_V_DIM = 128
DA_WIDTH = 512
ROT_DIM = 16
ROPE_THETA = 500000.0

SSM_GROUP = 16
SSM_GROUPS = 16
SSM_WIDTH = 256
SSM_STATE = 64

RW_HEADS = 4
RW_HEAD_DIM = 64
RW_WIDTH = 256
RW_DECAY_RANK = 32
RW_AAA_RANK = 32
RW_GATE_RANK = 64
RW_LN_EPS = 64e-5
RW_COLS = 896

DA_COLS = 1536

PEER_HEADS = 8
N_KEYS = 128
PEER_KEY_DIM = 128
PEER_TOPK = 16
PEER_SLOTS = PEER_HEADS * PEER_TOPK
PEER_ROW_WORDS = D_MODEL // 2
PEER_ROW_SUBLANES = PEER_ROW_WORDS // 128

LANES = 128
SUBLANES = 8
VMEM_LIMIT_BYTES = 56 * 1024 * 1024

F32 = jnp.float32
BF16 = jnp.bfloat16


def _cparams(n_axes, vmem=VMEM_LIMIT_BYTES):
    return pltpu.CompilerParams(
        dimension_semantics=("arbitrary",) * n_axes, vmem_limit_bytes=vmem)


def _dot(a, b):
    return jnp.dot(a, b, preferred_element_type=F32)


def _split(x):
    hi = x.astype(BF16)
    lo = (x - hi.astype(F32)).astype(BF16)
    return hi, lo


def _dot_x2(x, w):
    hi, lo = _split(x)
    return _dot(hi, w) + _dot(lo, w)


def _dot_x2k(x, w_stacked):
    hi, lo = _split(x)
    return _dot(jnp.concatenate([hi, lo], axis=1), w_stacked)


def _dot_x3(x, w_hi, w_lo):
    hi, lo = _split(x)
    return _dot(hi, w_hi) + _dot(lo, w_hi) + _dot(hi, w_lo)


def _wsplit(w):
    hi = w.astype(BF16)
    lo = (w - hi.astype(F32)).astype(BF16)
    return hi, lo


def _sigmoid(x):
    return 1.0 / (1.0 + jnp.exp(-x))


def _gelu(x):
    return 0.5 * x * (1.0 + lax.erf(x * (1.0 / math.sqrt(2.0))))


def _seg_ones(n, seg):
    r = jnp.arange(n) // seg
    return (r[:, None] == r[None, :]).astype(BF16)


def _full(shape):
    nd = len(shape)
    return pl.BlockSpec(shape, lambda *_: (0,) * nd)


def _inproj_body(x_ref, g_ref, wa_ref, wb_ref, wc_ref, a_ref, b_ref, c_ref):
    x = x_ref[...]
    ms = jnp.mean(x * x, axis=-1, keepdims=True)
    h = (x * lax.rsqrt(ms + NORM_EPS) * g_ref[...]).astype(BF16)
    a_ref[...] = _dot(h, wa_ref[...])
    b_ref[...] = _dot(h, wb_ref[...])
    c_ref[...] = _dot(h, wc_ref[...])


def _inproj(x, g, wa, wb, wc, tl):
    bsz, seq, _ = x.shape
    return pl.pallas_call(
        _inproj_body,
        grid=(bsz, seq // tl),
        in_specs=[
            pl.BlockSpec((None, tl, D_MODEL), lambda b, i: (b, i, 0)),
            _full((1, D_MODEL)),
            _full((D_MODEL, DA_COLS)),
            _full((D_MODEL, SSM_WIDTH)),
            _full((D_MODEL, RW_COLS)),
        ],
        out_specs=[
            pl.BlockSpec((None, tl, DA_COLS), lambda b, i: (b, i, 0)),
            pl.BlockSpec((tl, SSM_WIDTH), lambda b, i: (i, b)),
            pl.BlockSpec((tl, RW_COLS), lambda b, i: (i, b)),
        ],
        out_shape=[
            jax.ShapeDtypeStruct((bsz, seq, DA_COLS), F32),
            jax.ShapeDtypeStruct((seq, bsz * SSM_WIDTH), F32),
            jax.ShapeDtypeStruct((seq, bsz * RW_COLS), F32),
        ],
        compiler_params=_cparams(2),
        name="inproj",
    )(x, g, wa, wb, wc)


def _rope_body(pos_ref, invf_ref, c_ref, s1_ref, s2_ref):
    ang = pos_ref[...].astype(F32) * invf_ref[...]
    c, s = jnp.cos(ang), jnp.sin(ang)
    lm = lax.broadcasted_iota(jnp.int32, ang.shape, 1) % DA_HEAD_DIM
    half = ROT_DIM // 2
    c_ref[...] = jnp.where(lm < ROT_DIM, c, 1.0)
    s1_ref[...] = jnp.where(lm < half, -s, 0.0)
    s2_ref[...] = jnp.where((lm >= half) & (lm < ROT_DIM), s, 0.0)


def _rope_tables(positions, tm):
    t = positions.size
    half = ROT_DIM // 2
    inv_freq = ROPE_THETA ** (-jnp.arange(0, ROT_DIM, 2, dtype=F32) / ROT_DIM)
    invf = inv_freq[jnp.arange(LANES) % half][None, :]
    pos = jnp.broadcast_to(positions.reshape(t, 1), (t, LANES))
    spec = pl.BlockSpec((tm, LANES), lambda i: (i, 0))
    return pl.pallas_call(
        _rope_body,
        grid=(t // tm,),
        in_specs=[spec, _full((1, LANES))],
        out_specs=[spec, spec, spec],
        out_shape=[jax.ShapeDtypeStruct((t, LANES), F32)] * 3,
        compiler_params=_cparams(1),
        name="rope_tables",
    )(pos, invf)


def _attn_prep_body(a_ref, c_ref, s1_ref, s2_ref, qg_ref, kg_ref, seg_ref,
                    q_ref, k_ref, v_ref):
    reps = DA_WIDTH // LANES
    c = jnp.tile(c_ref[...], (1, reps))
    s1 = jnp.tile(s1_ref[...], (1, reps))
    s2 = jnp.tile(s2_ref[...], (1, reps))
    seg = seg_ref[...]
    half = ROT_DIM // 2

    def norm_rope(t, gain):
        ms = _dot_x2(t * t, seg) * (1.0 / DA_HEAD_DIM)
        tn = t * lax.rsqrt(ms + NORM_EPS) * gain
        up = pltpu.roll(tn, DA_WIDTH - half, axis=1)
        dn = pltpu.roll(tn, half, axis=1)
        return tn * c + up * s1 + dn * s2

    q = norm_rope(a_ref[:, 0:DA_WIDTH], qg_ref[...])
    k = norm_rope(a_ref[:, DA_WIDTH:2 * DA_WIDTH], kg_ref[...])
    q_ref[...] = (q * (DA_HEAD_DIM ** -0.5)).astype(BF16)
    k_ref[...] = k.astype(BF16)
    v_ref[...] = a_ref[:, 2 * DA_WIDTH:3 * DA_WIDTH].astype(BF16)


def _attn_prep(cols_a, rope, qg, kg, tm):
    t = cols_a.shape[0]
    c, s1, s2 = rope
    lane_spec = pl.BlockSpec((tm, LANES), lambda i: (i, 0))
    out_spec = pl.BlockSpec((tm, DA_WIDTH), lambda i: (i, 0))
    return pl.pallas_call(
        _attn_prep_body,
        grid=(t // tm,),
        in_specs=[
            pl.BlockSpec((tm, DA_COLS), lambda i: (i, 0)),
            lane_spec, lane_spec, lane_spec,
            _full((1, DA_WIDTH)), _full((1, DA_WIDTH)),
            _full((DA_WIDTH, DA_WIDTH)),
        ],
        out_specs=[out_spec, out_spec, out_spec],
        out_shape=[jax.ShapeDtypeStruct((t, DA_WIDTH), BF16)] * 3,
        compiler_params=_cparams(1),
        name="attn_prep",
    )(cols_a, c, s1, s2, qg, kg, _seg_ones(DA_WIDTH, DA_HEAD_DIM))


def _attn_body(q_ref, k_ref, v_ref, lam_ref, sub_ref, o_ref, m_ref, l_ref, acc_ref, s_ref, p_ref,
               *, tq, tw):
    qi = pl.program_id(2)
    q = q_ref[...]
    lane = lax.broadcasted_iota(jnp.int32, q.shape, 1)
    zero = jnp.zeros_like(q)
    qmaps = (jnp.where(lane < DA_HEAD_DIM, q, zero), jnp.where(lane >= DA_HEAD_DIM, q, zero))

    m_ref[...] = jnp.full(m_ref.shape, -jnp.inf, F32)
    l_ref[...] = jnp.zeros(l_ref.shape, F32)
    acc_ref[...] = jnp.zeros(acc_ref.shape, F32)

    def span(c0, masked):
        c0 = pl.multiple_of(c0, tw)
        k = k_ref[pl.ds(c0, tw), :]
        v = v_ref[pl.ds(c0, tw), :]
        if masked:
            row_chunk = (lax.broadcasted_iota(jnp.int32, (tq, tw), 0) + tq) // CHUNK
            col_chunk = (lax.broadcasted_iota(jnp.int32, (tq, tw), 1) + (c0 - qi * tq + tq)) // CHUNK
            visible = col_chunk <= row_chunk
        for m in range(2):
            s = lax.dot_general(qmaps[m], k, (((1,), (1,)), ((), ())),
                                preferred_element_type=F32)
            s_ref[m] = jnp.where(visible, s, -jnp.inf) if masked else s
        for m in range(2):
            s = s_ref[m]
            m_old = m_ref[m]
            m_new = jnp.maximum(m_old, jnp.max(s, axis=1, keepdims=True))
            alpha = jnp.exp(m_old - m_new)
            p = jnp.exp(s - jnp.tile(m_new, (1, tw // LANES)))
            p_ref[m] = p.astype(BF16)
            l_ref[m] = alpha * l_ref[m] + jnp.sum(p, axis=1, keepdims=True)
            acc_ref[m] = alpha * acc_ref[m]
            m_ref[m] = m_new
        for m in range(2):
            acc_ref[m] += _dot(p_ref[m], v)

    def full_span(c, carry):
        span(c * tw, False)
        return carry

    n_full = qi // (tw // tq)
    lax.fori_loop(0, n_full, full_span, 0)
    span(n_full * tw, True)

    lam = lam_ref[...]
    lam_full = (jnp.exp(jnp.sum(lam[0:1] * lam[1:2], axis=1, keepdims=True))
                - jnp.exp(jnp.sum(lam[2:3] * lam[3:4], axis=1, keepdims=True))
                + lam[4:5, 0:1])
    lam_init = lam[4:5, 0:1]
    o = acc_ref[0] / l_ref[0] - lam_full * (acc_ref[1] / l_ref[1])
    ms = jnp.mean(o * o, axis=-1, keepdims=True)
    o_ref[...] = o * lax.rsqrt(ms + NORM_EPS) * sub_ref[...] * (1.0 - lam_init)


def _attention(q, k, v, lam_pad, subln, tq):
    bsz, seq, _ = q.shape
    tw = min(2 * tq, seq)
    kv_spec = pl.BlockSpec((None, seq, DA_V_DIM), lambda b, h, i: (b, 0, h))
    return pl.pallas_call(
        functools.partial(_attn_body, tq=tq, tw=tw),
        grid=(bsz, DA_HEADS, seq // tq),
        in_specs=[
            pl.BlockSpec((None, tq, DA_V_DIM), lambda b, h, i: (b, i, h)),
            kv_spec, kv_spec,
            _full((SUBLANES, LANES)),
            _full((1, DA_V_DIM)),
        ],
        out_specs=pl.BlockSpec((None, tq, DA_V_DIM), lambda b, h, i: (b, i, h)),
        out_shape=jax.ShapeDtypeStruct((bsz, seq, DA_WIDTH), F32),
        scratch_shapes=[
            pltpu.VMEM((2, tq, LANES), F32),
            pltpu.VMEM((2, tq, LANES), F32),
            pltpu.VMEM((2, tq, DA_V_DIM), F32),
            pltpu.VMEM((2, tq, tw), F32),
            pltpu.VMEM((2, tq, tw), BF16),
        ],
        compiler_params=_cparams(3),
        name="diff_attention",
    )(q, k, v, lam_pad, subln)


def _s5_body(u_ref, bhi_ref, blo_ref, are_ref, aim_ref, chi_ref, clo_ref, d_ref,
             whi_ref, wlo_ref, bg_ref, g_ref, o_ref, st_ref, x_ref, *, tt, bsz):
    ns = SSM_GROUPS * SSM_STATE

    @pl.when(pl.program_id(0) == 0)
    def _():
        st_ref[...] = jnp.zeros(st_ref.shape, F32)

    u = u_ref[...]
    x_ref[...] = _dot_x3(u, bhi_ref[...], blo_ref[...])
    a_re = jnp.broadcast_to(are_ref[...], (bsz, ns))
    a_im = jnp.broadcast_to(aim_ref[...], (bsz, ns))

    def step(t, carry):
        xr, xi = carry
        r = pl.multiple_of(t * bsz, bsz)
        nr = a_re * xr - a_im * xi + x_ref[pl.ds(r, bsz), 0:ns]
        ni = a_re * xi + a_im * xr + x_ref[pl.ds(r, bsz), ns:2 * ns]
        x_ref[pl.ds(r, bsz), 0:ns] = nr
        x_ref[pl.ds(r, bsz), ns:2 * ns] = ni
        return nr, ni

    xr, xi = lax.fori_loop(0, tt, step, (st_ref[:, 0:ns], st_ref[:, ns:2 * ns]))
    st_ref[:, 0:ns] = xr
    st_ref[:, ns:2 * ns] = xi

    y = _dot_x3(x_ref[...], chi_ref[...], clo_ref[...]) + d_ref[...] * u
    yg = _gelu(y)
    z = _dot_x3(yg, whi_ref[...], wlo_ref[...]) + bg_ref[...]
    out = yg * _sigmoid(z)
    ms = jnp.mean(out * out, axis=-1, keepdims=True)
    o_ref[...] = out * lax.rsqrt(ms + NORM_EPS) * g_ref[...]


def _s5(u, ops, tt, bsz):
    rows = u.shape[0]
    ns2 = 2 * SSM_GROUPS * SSM_STATE
    blk = tt * bsz
    io_spec = pl.BlockSpec((blk, SSM_WIDTH), lambda i: (i, 0))
    return pl.pallas_call(
        functools.partial(_s5_body, tt=tt, bsz=bsz),
        grid=(rows // blk,),
        in_specs=[
            io_spec,
            _full((SSM_WIDTH, ns2)), _full((SSM_WIDTH, ns2)),
            _full((1, ns2 // 2)), _full((1, ns2 // 2)),
            _full((ns2, SSM_WIDTH)), _full((ns2, SSM_WIDTH)),
            _full((1, SSM_WIDTH)),
            _full((SSM_WIDTH, SSM_WIDTH)), _full((SSM_WIDTH, SSM_WIDTH)),
            _full((1, SSM_WIDTH)), _full((1, SSM_WIDTH)),
        ],
        out_specs=io_spec,
        out_shape=jax.ShapeDtypeStruct((rows, SSM_WIDTH), F32),
        scratch_shapes=[
            pltpu.VMEM((bsz, ns2), F32),
            pltpu.VMEM((blk, ns2), F32),
        ],
        compiler_params=_cparams(1),
        name="s5",
    )(u, *ops)


def _s5_operands(lam_re, lam_im, b_re, b_im, c_re, c_im, d_skip, log_dt, w_glu, b_glu, out_norm):
    lr = jnp.minimum(lam_re, -1e-4)
    li = lam_im
    step = jnp.exp(log_dt)[:, None]
    mag = jnp.exp(lr * step)
    abar_re = mag * jnp.cos(li * step)
    abar_im = mag * jnp.sin(li * step)
    den = lr * lr + li * li
    coef_re = ((abar_re - 1.0) * lr + abar_im * li) / den
    coef_im = (abar_im * lr - (abar_re - 1.0) * li) / den
    bbar_re = coef_re[..., None] * b_re - coef_im[..., None] * b_im
    bbar_im = coef_re[..., None] * b_im + coef_im[..., None] * b_re
    eye = jnp.eye(SSM_GROUPS, dtype=F32)
    ns = SSM_GROUPS * SSM_STATE

    def in_mat(bb):
        return jnp.einsum('gph,gk->ghkp', bb, eye).reshape(SSM_WIDTH, ns)

    def out_mat(cc):
        return jnp.einsum('ghp,gk->kpgh', cc, eye).reshape(ns, SSM_WIDTH)

    bmat = jnp.concatenate([in_mat(bbar_re), in_mat(bbar_im)], axis=1)
    cmat = jnp.concatenate([out_mat(c_re), -out_mat(c_im)], axis=0)
    return (*_wsplit(bmat), abar_re.reshape(1, ns), abar_im.reshape(1, ns), *_wsplit(cmat),
            d_skip.reshape(1, SSM_WIDTH), *_wsplit(w_glu), b_glu.reshape(1, SSM_WIDTH),
            out_norm.reshape(1, SSM_WIDTH))


def _rwkv_body(c_ref, mu_ref, wdh_ref, wdl_ref, wah_ref, wal_ref, wgh_ref, wgl_ref,
               w0_ref, a0_ref, kk_ref, ka_ref, rk_ref, lnw_ref, lnb_ref,
               seg256_ref, seg128_ref, eye_ref,
               o_ref, prev_ref, st_ref, r_s, w_s, k_s, v_s, a_s, b_s, y_s, *, tt, bsz):
    n_pairs = RW_WIDTH // LANES
    n_inst = n_pairs * bsz
    hd = RW_HEAD_DIM

    @pl.when(pl.program_id(0) == 0)
    def _():
        prev_ref[...] = jnp.zeros(prev_ref.shape, F32)
        st_ref[...] = jnp.zeros(st_ref.shape, F32)

    c = c_ref[...]
    rows = c.shape[0]
    prev = jnp.concatenate([prev_ref[...], c[0:rows - bsz]], axis=0)
    prev_ref[...] = c[rows - bsz:rows]
    cs = c + (prev - c) * mu_ref[...]
    r = cs[:, 0:RW_WIDTH]
    k = cs[:, RW_WIDTH:2 * RW_WIDTH]
    v = cs[:, 2 * RW_WIDTH:3 * RW_WIDTH]
    x3 = cs[:, 3 * RW_WIDTH:RW_COLS]

    seg256 = seg256_ref[...]
    wdec = w0_ref[...] + _dot_x3(jnp.tanh(x3), wdh_ref[...], wdl_ref[...])
    z = -wdec
    softplus = jnp.maximum(z, 0.0) + jnp.log(1.0 + jnp.exp(-jnp.abs(z)))
    decay = jnp.exp(-jnp.exp(-softplus - 0.5))
    a = _sigmoid(a0_ref[...] + _dot_x3(x3, wah_ref[...], wal_ref[...]))
    g = _dot_x3(_sigmoid(x3), wgh_ref[...], wgl_ref[...])
    kk = k * kk_ref[...]
    kk = kk / jnp.maximum(jnp.sqrt(_dot_x2(kk * kk, seg256)), 1e-12)
    k2 = k * (1.0 + (a - 1.0) * ka_ref[...])
    r_s[...] = r
    w_s[...] = decay
    k_s[...] = k2
    v_s[...] = v
    a_s[...] = -kk
    b_s[...] = kk * a

    seg128x2 = seg128_ref[...]
    seg128 = seg128x2[0:LANES]
    eye = eye_ref[...]
    sub = lax.broadcasted_iota(jnp.int32, (bsz, LANES), 0)

    def vec_of(block, inst):
        p, b = divmod(inst, bsz)
        return block[b:b + 1, p * LANES:(p + 1) * LANES]

    def read_out(states, r_blk, dst_row):
        lhs_y = jnp.concatenate([states[i] * vec_of(r_blk, i) for i in range(n_inst)], axis=0)
        ycol = _dot(lhs_y.astype(BF16), seg128)
        for p in range(n_pairs):
            yblk = jnp.zeros((bsz, LANES), F32)
            for b in range(bsz):
                i = p * bsz + b
                yrow = jnp.sum(eye * ycol[i * hd:(i + 1) * hd], axis=0, keepdims=True)
                yblk = jnp.where(sub == b, yrow, yblk)
            y_s[pl.ds(dst_row, bsz), p * LANES:(p + 1) * LANES] = yblk

    def step(t, carry):
        row0 = pl.multiple_of(t * bsz, bsz)
        blk = [ref[pl.ds(row0, bsz), :] for ref in (w_s, k_s, v_s, a_s, b_s)]
        vec = lambda q, inst: vec_of(blk[q], inst)
        states = [st_ref[i * hd:(i + 1) * hd, :] for i in range(n_inst)]
        prev_row = pl.multiple_of(jnp.maximum(row0 - bsz, 0), bsz)
        read_out(states, r_s[pl.ds(prev_row, bsz), :], row0)
        lhs_a = jnp.concatenate([states[i] * vec(3, i) for i in range(n_inst)], axis=0)
        lhs_v = jnp.concatenate([eye * vec(2, i) for i in range(n_inst)], axis=0)
        sa = _dot_x2k(lhs_a, seg128x2)
        vcol = _dot_x2k(lhs_v, seg128x2)
        for i in range(n_inst):
            sl = slice(i * hd, (i + 1) * hd)
            st_ref[sl, :] = states[i] * vec(0, i) + sa[sl] * vec(4, i) + vcol[sl] * vec(1, i)
        return carry

    lax.fori_loop(0, tt, step, 0)
    last = (tt - 1) * bsz
    read_out([st_ref[i * hd:(i + 1) * hd, :] for i in range(n_inst)], r_s[last:last + bsz, :], tt * bsz)

    y = y_s[bsz:(tt + 1) * bsz, :]
    inv = 1.0 / hd
    mean = _dot_x2(y, seg256) * inv
    dv = y - mean
    var = _dot_x2(dv * dv, seg256) * inv
    yn = dv * lax.rsqrt(var + RW_LN_EPS) * lnw_ref[...] + lnb_ref[...]
    bonus = _dot_x2(r * k2 * rk_ref[...], seg256) * v
    o_ref[...] = (yn + bonus) * g


def _rwkv(c, ops, tt, bsz):
    rows = c.shape[0]
    blk = tt * bsz
    vec_spec = _full((1, RW_WIDTH))
    pad_spec = _full((LANES, RW_WIDTH))
    n_inst = (RW_WIDTH // LANES) * bsz
    return pl.pallas_call(
        functools.partial(_rwkv_body, tt=tt, bsz=bsz),
        grid=(rows // blk,),
        in_specs=[
            pl.BlockSpec((blk, RW_COLS), lambda i: (i, 0)),
            _full((1, RW_COLS)),
            pad_spec, pad_spec, pad_spec, pad_spec, pad_spec, pad_spec,
            vec_spec, vec_spec, vec_spec, vec_spec, vec_spec, vec_spec, vec_spec,
            _full((RW_WIDTH, RW_WIDTH)), _full((2 * LANES, LANES)), _full((RW_HEAD_DIM, LANES)),
        ],
        out_specs=pl.BlockSpec((blk, RW_WIDTH), lambda i: (i, 0)),
        out_shape=jax.ShapeDtypeStruct((rows, RW_WIDTH), F32),
        scratch_shapes=[
            pltpu.VMEM((bsz, RW_COLS), F32),
            pltpu.VMEM((n_inst * RW_HEAD_DIM, LANES), F32),
        ] + [pltpu.VMEM((blk, RW_WIDTH), F32)] * 6 + [pltpu.VMEM((blk + bsz, RW_WIDTH), F32)],
        compiler_params=_cparams(1),
        name="rwkv7",
    )(c, *ops)


def _rwkv_operands(mu, decay_up, decay_w0, aaa_up, aaa_a0, gate_up, k_k, k_a, r_k, ln_w, ln_b):
    def pad(w, off):
        return _wsplit(jnp.zeros((LANES, RW_WIDTH), F32).at[off:off + w.shape[0]].set(w))

    row = lambda t: t.reshape(1, RW_WIDTH)
    lane = jnp.arange(LANES)
    eye = (lane[None, :] % RW_HEAD_DIM == jnp.arange(RW_HEAD_DIM)[:, None]).astype(F32)
    return (mu.reshape(1, RW_COLS),
            *pad(decay_up, 0), *pad(aaa_up, RW_DECAY_RANK),
            *pad(gate_up, RW_DECAY_RANK + RW_AAA_RANK),
            row(decay_w0), row(aaa_a0), row(k_k), row(k_a), row(r_k), row(ln_w), row(ln_b),
            _seg_ones(RW_WIDTH, RW_HEAD_DIM),
            jnp.concatenate([_seg_ones(LANES, RW_HEAD_DIM)] * 2, axis=0), eye)


def _outproj_body(x_ref, ya_ref, yb_ref, yc_ref, wa_ref, wb_ref, wc_ref, g_ref, wq_ref,
                  keys_ref, xn_ref, h_ref, st_ref):
    xn = (x_ref[...]
          + _dot(ya_ref[...].astype(BF16), wa_ref[...])
          + _dot(yb_ref[...].astype(BF16), wb_ref[...])
          + _dot(yc_ref[...].astype(BF16), wc_ref[...]))
    xn_ref[...] = xn
    ms = jnp.mean(xn * xn, axis=-1, keepdims=True)
    h = xn * lax.rsqrt(ms + NORM_EPS) * g_ref[...]
    h_ref[...] = h
    q = _dot(h.astype(BF16), wq_ref[...]).astype(BF16)
    for hm in range(2 * PEER_HEADS):
        st_ref[hm * N_KEYS:(hm + 1) * N_KEYS, :] = lax.dot_general(
            keys_ref[hm], q[:, hm * PEER_KEY_DIM:(hm + 1) * PEER_KEY_DIM],
            (((1,), (1,)), ((), ())), preferred_element_type=F32)


def _outproj(x, ya, yb, yc, wa, wb, wc, g, wq, keys, tl):
    bsz, seq, _ = x.shape
    nl = seq // tl
    n_rows = 2 * PEER_HEADS * N_KEYS
    tok_spec = pl.BlockSpec((None, tl, D_MODEL), lambda b, i: (b, i, 0))
    tm_spec = pl.BlockSpec((tl, SSM_WIDTH), lambda b, i: (i, b))
    return pl.pallas_call(
        _outproj_body,
        grid=(bsz, nl),
        in_specs=[
            tok_spec,
            pl.BlockSpec((None, tl, DA_WIDTH), lambda b, i: (b, i, 0)),
            tm_spec, tm_spec,
            _full((DA_WIDTH, D_MODEL)), _full((SSM_WIDTH, D_MODEL)), _full((RW_WIDTH, D_MODEL)),
            _full((1, D_MODEL)),
            _full((D_MODEL, n_rows)),
            _full((2 * PEER_HEADS, N_KEYS, PEER_KEY_DIM)),
        ],
        out_specs=[
            tok_spec, tok_spec,
            pl.BlockSpec((n_rows, tl), lambda b, i: (0, b * nl + i)),
        ],
        out_shape=[
            jax.ShapeDtypeStruct((bsz, seq, D_MODEL), F32),
            jax.ShapeDtypeStruct((bsz, seq, D_MODEL), F32),
            jax.ShapeDtypeStruct((n_rows, bsz * seq), F32),
        ],
        compiler_params=_cparams(2),
        name="outproj_peer_scores",
    )(x, ya, yb, yc, wa, wb, wc, g, wq, keys)


def _topk_rows(s, k, payload=None):
    n, width = s.shape
    rows = lax.broadcasted_iota(jnp.int32, s.shape, 0)
    out_rows = lax.broadcasted_iota(jnp.int32, (k, width), 0)
    vals = jnp.zeros((k, width), F32)
    picks = jnp.zeros((k, width), jnp.int32)
    for j in range(k):
        m = jnp.max(s, axis=0, keepdims=True)
        pos = jnp.min(jnp.where(s == m, rows, n), axis=0, keepdims=True)
        hit = rows == pos
        pick = pos if payload is None else jnp.max(jnp.where(hit, payload, -1), axis=0, keepdims=True)
        vals = jnp.where(out_rows == j, m, vals)
        picks = jnp.where(out_rows == j, pick, picks)
        s = jnp.where(hit, -jnp.inf, s)
    return vals, picks


def _candidates(t1, t2, combine, fill):
    k = PEER_TOPK
    sub = lax.broadcasted_iota(jnp.int32, (SUBLANES, t1.shape[1]), 0)
    blocks = [combine(t1[0:1], t2)]
    for a in range(1, SUBLANES):
        n = k // (a + 1)
        blk = combine(t1[a:a + 1], t2[0:SUBLANES])
        blocks.append(blk if n >= SUBLANES else jnp.where(sub < n, blk, fill))
    blocks.append(combine(t1[SUBLANES:k], t2[0:1]))
    return jnp.concatenate(blocks, axis=0)


def _topk_body(st_ref, idx_ref, gate_ref):
    idx_rows, gate_rows = [], []
    for h in range(PEER_HEADS):
        v1, i1 = _topk_rows(st_ref[(2 * h) * N_KEYS:(2 * h + 1) * N_KEYS, :], PEER_TOPK)
        v2, i2 = _topk_rows(st_ref[(2 * h + 1) * N_KEYS:(2 * h + 2) * N_KEYS, :], PEER_TOPK)
        cand_s = _candidates(v1, v2, lambda x, y: x + y, -jnp.inf)
        cand_e = _candidates(i1, i2, lambda x, y: x * N_KEYS + y, -1)
        best, expert = _topk_rows(cand_s, PEER_TOPK, payload=cand_e)
        e = jnp.exp(best - best[0:1])
        gate_rows.append(e / jnp.sum(e, axis=0, keepdims=True))
        idx_rows.append(expert * PEER_ROW_SUBLANES)
    gate_ref[...] = jnp.concatenate(gate_rows, axis=0)
    idx_ref[...] = jnp.concatenate(idx_rows, axis=0).T


def _topk(scores_t, tb):
    t = scores_t.shape[1]
    return pl.pallas_call(
        _topk_body,
        grid=(t // tb,),
        in_specs=[pl.BlockSpec((scores_t.shape[0], tb), lambda i: (0, i))],
        out_specs=[
            pl.BlockSpec((tb, PEER_SLOTS), lambda i: (i, 0)),
            pl.BlockSpec((PEER_SLOTS, tb), lambda i: (0, i)),
        ],
        out_shape=[
            jax.ShapeDtypeStruct((t, PEER_SLOTS), jnp.int32),
            jax.ShapeDtypeStruct((PEER_SLOTS, t), F32),
        ],
        compiler_params=_cparams(1),
        name="peer_topk",
    )(scores_t)


def _pack_body(t_ref, o_ref):
    n = t_ref.shape[0]
    bits = lax.bitcast_convert_type(t_ref[...].astype(BF16).astype(F32), jnp.uint32)
    words = (bits[:, :PEER_ROW_WORDS] >> 16) | (bits[:, PEER_ROW_WORDS:] & jnp.uint32(0xFFFF0000))
    for r in range(PEER_ROW_SUBLANES):
        o_ref[pl.ds(r, n, stride=PEER_ROW_SUBLANES), :] = words[:, r * LANES:(r + 1) * LANES]


def _pack_table(tab, te=512):
    n = tab.shape[0]
    return pl.pallas_call(
        _pack_body,
        grid=(n // te,),
        in_specs=[pl.BlockSpec((te, D_MODEL), lambda i: (i, 0))],
        out_specs=pl.BlockSpec((te * PEER_ROW_SUBLANES, LANES), lambda i: (i, 0)),
        out_shape=jax.ShapeDtypeStruct((n * PEER_ROW_SUBLANES, LANES), jnp.uint32),
        compiler_params=_cparams(1),
        name="peer_pack_table",
    )(tab)


PEER_TILE_ROWS = PEER_SLOTS * PEER_ROW_SUBLANES
PEER_UNROLL = SUBLANES


def _gather_rows(tab_ref, idx_ref, tt, dst_ref, u):
    rs = PEER_ROW_SUBLANES
    for k in range(PEER_SLOTS):
        off = pl.multiple_of(idx_ref[tt, k], rs)
        dst_ref[pl.ds(u * PEER_TILE_ROWS + k * rs, rs), :] = tab_ref[pl.ds(off, rs), :]


def _row_plane(rows_ref, u, r):
    words = rows_ref[pl.ds(u * PEER_TILE_ROWS + r, PEER_SLOTS, stride=PEER_ROW_SUBLANES), :]
    lo = lax.bitcast_convert_type(words << 16, F32)
    hi = lax.bitcast_convert_type(words & jnp.uint32(0xFFFF0000), F32)
    return lo, hi


def _table_spec(tab):
    return pl.BlockSpec(tab.shape, lambda i: (0, 0), pipeline_mode=pl.Buffered(1))


def _peer_act_body(idx_ref, h_ref, gate_ref, tab_ref, coef_ref, rows_ref, act_ref, *, tb):
    lane = lax.broadcasted_iota(jnp.int32, (PEER_SLOTS, tb), 1)
    half = D_MODEL // 2
    act_ref[...] = jnp.zeros(act_ref.shape, F32)

    def group(g, carry):
        row0 = pl.multiple_of(g * PEER_UNROLL, PEER_UNROLL)
        h8 = h_ref[pl.ds(row0, PEER_UNROLL), :]
        for u in range(PEER_UNROLL):
            _gather_rows(tab_ref, idx_ref, row0 + u, rows_ref, u)
            part = jnp.zeros((PEER_SLOTS, LANES), F32)
            for r in range(PEER_ROW_SUBLANES):
                lo, hi = _row_plane(rows_ref, u, r)
                part = (part + lo * h8[u:u + 1, r * LANES:(r + 1) * LANES]
                        + hi * h8[u:u + 1, half + r * LANES:half + (r + 1) * LANES])
            act = jnp.sum(part, axis=1, keepdims=True)
            act_ref[...] = jnp.where(lane == row0 + u, act, act_ref[...])
        return carry

    lax.fori_loop(0, tb // PEER_UNROLL, group, 0)
    coef_ref[...] = _gelu(act_ref[...]) * gate_ref[...]


def _peer_act(idx, h, gate_t, tab, tb):
    t = idx.shape[0]
    slot_major = pl.BlockSpec((PEER_SLOTS, tb), lambda i: (0, i))
    return pl.pallas_call(
        functools.partial(_peer_act_body, tb=tb),
        grid=(t // tb,),
        in_specs=[
            pl.BlockSpec((tb, PEER_SLOTS), lambda i: (i, 0), memory_space=pltpu.SMEM),
            pl.BlockSpec((tb, D_MODEL), lambda i: (i, 0)),
            slot_major,
            _table_spec(tab),
        ],
        out_specs=slot_major,
        out_shape=jax.ShapeDtypeStruct((PEER_SLOTS, t), F32),
        scratch_shapes=[
            pltpu.VMEM((PEER_UNROLL * PEER_TILE_ROWS, LANES), jnp.uint32),
            pltpu.VMEM((PEER_SLOTS, tb), F32),
        ],
        compiler_params=_cparams(1),
        name="peer_act",
    )(idx, h, gate_t, tab)


def _peer_out_body(idx_ref, coef_ref, x_ref, ones_ref, tab_ref, o_ref, rows_ref, *, tb):
    lane = lax.broadcasted_iota(jnp.int32, (PEER_SLOTS, tb), 1)
    sub = lax.broadcasted_iota(jnp.int32, (PEER_UNROLL, LANES), 0)
    ones = ones_ref[...]
    n_planes = PEER_ROW_SUBLANES

    def group(g, carry):
        row0 = pl.multiple_of(g * PEER_UNROLL, PEER_UNROLL)
        planes = [jnp.zeros((PEER_UNROLL, LANES), F32) for _ in range(2 * n_planes)]
        for u in range(PEER_UNROLL):
            _gather_rows(tab_ref, idx_ref, row0 + u, rows_ref, u)
            ccol = _dot_x2k(jnp.where(lane == row0 + u, coef_ref[...], 0.0), ones)
            for r in range(n_planes):
                lo, hi = _row_plane(rows_ref, u, r)
                planes[r] = jnp.where(sub == u, jnp.sum(lo * ccol, axis=0, keepdims=True), planes[r])
                planes[n_planes + r] = jnp.where(
                    sub == u, jnp.sum(hi * ccol, axis=0, keepdims=True), planes[n_planes + r])
        o_ref[pl.ds(row0, PEER_UNROLL), :] = (x_ref[pl.ds(row0, PEER_UNROLL), :]
                                              + jnp.concatenate(planes, axis=1))
        return carry

    lax.fori_loop(0, tb // PEER_UNROLL, group, 0)


def _peer_out(idx, coef_t, x, tab, tb):
    t = idx.shape[0]
    tok_spec = pl.BlockSpec((tb, D_MODEL), lambda i: (i, 0))
    ones = jnp.ones((2 * tb, LANES), BF16)
    return pl.pallas_call(
        functools.partial(_peer_out_body, tb=tb),
        grid=(t // tb,),
        in_specs=[
            pl.BlockSpec((tb, PEER_SLOTS), lambda i: (i, 0), memory_space=pltpu.SMEM),
            pl.BlockSpec((PEER_SLOTS, tb), lambda i: (0, i)),
            tok_spec,
            _full(ones.shape),
            _table_spec(tab),
        ],
        out_specs=tok_spec,
        out_shape=jax.ShapeDtypeStruct(x.shape, F32),
        scratch_shapes=[pltpu.VMEM((PEER_UNROLL * PEER_TILE_ROWS, LANES), jnp.uint32)],
        compiler_params=_cparams(1),
        name="peer_out",
    )(idx, coef_t, x, ones, tab)


def _tiles(bsz, seq):
    tl = min(seq, 512)
    tq = min(seq, 256)
    tt_s5 = min(seq, 128)
    tt_rw = min(seq, 64)
    tb = 128
    return tl, tq, tt_s5, tt_rw, tb


def kernel(x, positions, norm_mix, w_in, da_q_norm, da_k_norm, da_lambda, da_subln, ssm_lambda_re, ssm_lambda_im, ssm_b_re, ssm_b_im, ssm_c_re, ssm_c_im, ssm_d, ssm_log_dt, ssm_w_glu, ssm_b_glu, ssm_out_norm, rw_mu, rw_decay_up, rw_decay_w0, rw_aaa_up, rw_aaa_a0, rw_gate_up, rw_k_k, rw_k_a, rw_r_k, rw_ln_w, rw_ln_b, w_out, norm_ffn, peer_w_q, peer_sub_keys, peer_u, peer_v):
    bsz, seq, _ = x.shape
    depth = w_in.shape[0]
    t = bsz * seq
    tl, tq, tt_s5, tt_rw, tb = _tiles(bsz, seq)
    rope = _rope_tables(positions, tl)

    for l in range(depth):
        w = w_in[l].astype(BF16)
        cols_a, cols_b, cols_c = _inproj(
            x, norm_mix[l].reshape(1, D_MODEL), w[:, :DA_COLS],
            w[:, DA_COLS:DA_COLS + SSM_WIDTH], w[:, DA_COLS + SSM_WIDTH:], tl)

        q, k, v = _attn_prep(cols_a.reshape(t, DA_COLS), rope,
                             da_q_norm[l].reshape(1, DA_WIDTH), da_k_norm[l].reshape(1, DA_WIDTH), tl)
        lam_init = 0.8 - 0.6 * math.exp(-0.3 * l)
        lam_pad = (jnp.zeros((SUBLANES, LANES), F32)
                   .at[0:4, 0:DA_HEAD_DIM].set(da_lambda[l]).at[4, :].set(lam_init))
        y_a = _attention(q.reshape(bsz, seq, DA_WIDTH), k.reshape(bsz, seq, DA_WIDTH),
                         v.reshape(bsz, seq, DA_WIDTH), lam_pad, da_subln[l].reshape(1, DA_V_DIM), tq)

        s5_ops = _s5_operands(ssm_lambda_re[l], ssm_lambda_im[l], ssm_b_re[l], ssm_b_im[l],
                              ssm_c_re[l], ssm_c_im[l], ssm_d[l], ssm_log_dt[l], ssm_w_glu[l],
                              ssm_b_glu[l], ssm_out_norm[l])
        y_b = _s5(cols_b.reshape(t, SSM_WIDTH), s5_ops, tt_s5, bsz)

        rw_ops = _rwkv_operands(rw_mu[l], rw_decay_up[l], rw_decay_w0[l], rw_aaa_up[l], rw_aaa_a0[l],
                                rw_gate_up[l], rw_k_k[l], rw_k_a[l], rw_r_k[l], rw_ln_w[l], rw_ln_b[l])
        y_c = _rwkv(cols_c.reshape(t, RW_COLS), rw_ops, tt_rw, bsz)

        wo = w_out[l].astype(BF16)
        keys = peer_sub_keys[l].reshape(2 * PEER_HEADS, N_KEYS, PEER_KEY_DIM).astype(BF16)
        x_mid, h_ffn, scores_t = _outproj(
            x, y_a, y_b.reshape(seq, bsz * SSM_WIDTH), y_c.reshape(seq, bsz * RW_WIDTH),
            wo[:DA_WIDTH], wo[DA_WIDTH:DA_WIDTH + SSM_WIDTH], wo[DA_WIDTH + SSM_WIDTH:],
            norm_ffn[l].reshape(1, D_MODEL), peer_w_q[l].astype(BF16), keys, tl)

        idx, gate_t = _topk(scores_t, tb)
        coef_t = _peer_act(idx, h_ffn.reshape(t, D_MODEL), gate_t, _pack_table(peer_u[l]), tb)
        x = _peer_out(idx, coef_t, x_mid.reshape(t, D_MODEL),
                      _pack_table(peer_v[l]), tb).reshape(bsz, seq, D_MODEL)
    return x
```

```python
import functools
import math

import jax
import jax.numpy as jnp
from jax import lax
from jax.experimental import pallas as pl
from jax.experimental.pallas import tpu as pltpu

D_MODEL = 1024
CHUNK = 64
NORM_EPS = 1e-6

DA_HEADS = 4
DA_HEAD_DIM = 64
DA_V_DIM = 128
DA_WIDTH = 512
DA_HEAD_GROUP = 4
ROT_DIM = 16
ROPE_THETA = 500000.0

SSM_GROUP = 16
SSM_GROUPS = 16
SSM_WIDTH = 256
SSM_STATE = 64

RW_HEADS = 4
RW_HEAD_DIM = 64
RW_WIDTH = 256
RW_DECAY_RANK = 32
RW_AAA_RANK = 32
RW_GATE_RANK = 64
RW_LN_EPS = 64e-5
RW_COLS = 896

DA_COLS = 1536

PEER_HEADS = 8
N_KEYS = 128
PEER_KEY_DIM = 128
PEER_TOPK = 16
PEER_SLOTS = PEER_HEADS * PEER_TOPK
PEER_ROW_WORDS = D_MODEL // 2
PEER_ROW_SUBLANES = PEER_ROW_WORDS // 128

LANES = 128
SUBLANES = 8
VMEM_LIMIT_BYTES = 56 * 1024 * 1024

F32 = jnp.float32
BF16 = jnp.bfloat16


def _cparams(n_axes, vmem=VMEM_LIMIT_BYTES):
    return pltpu.CompilerParams(
        dimension_semantics=("arbitrary",) * n_axes, vmem_limit_bytes=vmem)


def _dot(a, b):
    return jnp.dot(a, b, preferred_element_type=F32)


def _split(x):
    hi = x.astype(BF16)
    lo = (x - hi.astype(F32)).astype(BF16)
    return hi, lo


def _dot_x2(x, w):
    hi, lo = _split(x)
    return _dot(hi, w) + _dot(lo, w)


def _dot_x2k(x, w_stacked):
    hi, lo = _split(x)
    return _dot(jnp.concatenate([hi, lo], axis=1), w_stacked)


def _dot_x3(x, w_hi, w_lo):
    hi, lo = _split(x)
    return _dot(hi, w_hi) + _dot(lo, w_hi) + _dot(hi, w_lo)


def _wsplit(w):
    hi = w.astype(BF16)
    lo = (w - hi.astype(F32)).astype(BF16)
    return hi, lo


def _sigmoid(x):
    return 1.0 / (1.0 + jnp.exp(-x))


def _gelu(x):
    return 0.5 * x * (1.0 + lax.erf(x * (1.0 / math.sqrt(2.0))))


def _seg_ones(n, seg):
    r = jnp.arange(n) // seg
    return (r[:, None] == r[None, :]).astype(BF16)


def _full(shape):
    nd = len(shape)
    return pl.BlockSpec(shape, lambda *_: (0,) * nd)


def _inproj_body(x_ref, g_ref, wa_ref, wb_ref, wc_ref, a_ref, b_ref, c_ref):
    x = x_ref[...]
    ms = jnp.mean(x * x, axis=-1, keepdims=True)
    h = (x * lax.rsqrt(ms + NORM_EPS) * g_ref[...]).astype(BF16)
    a_ref[...] = _dot(h, wa_ref[...])
    b_ref[...] = _dot(h, wb_ref[...])
    c_ref[...] = _dot(h, wc_ref[...])


def _inproj(x, g, wa, wb, wc, tl):
    bsz, seq, _ = x.shape
    return pl.pallas_call(
        _inproj_body,
        grid=(bsz, seq // tl),
        in_specs=[
            pl.BlockSpec((None, tl, D_MODEL), lambda b, i: (b, i, 0)),
            _full((1, D_MODEL)),
            _full((D_MODEL, DA_COLS)),
            _full((D_MODEL, SSM_WIDTH)),
            _full((D_MODEL, RW_COLS)),
        ],
        out_specs=[
            pl.BlockSpec((None, tl, DA_COLS), lambda b, i: (b, i, 0)),
            pl.BlockSpec((tl, SSM_WIDTH), lambda b, i: (i, b)),
            pl.BlockSpec((tl, RW_COLS), lambda b, i: (i, b)),
        ],
        out_shape=[
            jax.ShapeDtypeStruct((bsz, seq, DA_COLS), F32),
            jax.ShapeDtypeStruct((seq, bsz * SSM_WIDTH), F32),
            jax.ShapeDtypeStruct((seq, bsz * RW_COLS), F32),
        ],
        compiler_params=_cparams(2),
        name="inproj",
    )(x, g, wa, wb, wc)


def _rope_body(pos_ref, invf_ref, c_ref, s1_ref, s2_ref):
    ang = pos_ref[...].astype(F32) * invf_ref[...]
    c, s = jnp.cos(ang), jnp.sin(ang)
    lm = lax.broadcasted_iota(jnp.int32, ang.shape, 1) % DA_HEAD_DIM
    half = ROT_DIM // 2
    c_ref[...] = jnp.where(lm < ROT_DIM, c, 1.0)
    s1_ref[...] = jnp.where(lm < half, -s, 0.0)
    s2_ref[...] = jnp.where((lm >= half) & (lm < ROT_DIM), s, 0.0)


def _rope_tables(positions, tm):
    t = positions.size
    half = ROT_DIM // 2
    inv_freq = ROPE_THETA ** (-jnp.arange(0, ROT_DIM, 2, dtype=F32) / ROT_DIM)
    invf = inv_freq[jnp.arange(LANES) % half][None, :]
    pos = jnp.broadcast_to(positions.reshape(t, 1), (t, LANES))
    spec = pl.BlockSpec((tm, LANES), lambda i: (i, 0))
    return pl.pallas_call(
        _rope_body,
        grid=(t // tm,),
        in_specs=[spec, _full((1, LANES))],
        out_specs=[spec, spec, spec],
        out_shape=[jax.ShapeDtypeStruct((t, LANES), F32)] * 3,
        compiler_params=_cparams(1),
        name="rope_tables",
    )(pos, invf)


def _attn_prep_body(a_ref, c_ref, s1_ref, s2_ref, qg_ref, kg_ref, seg_ref,
                    q_ref, k_ref, v_ref):
    reps = DA_WIDTH // LANES
    c = jnp.tile(c_ref[...], (1, reps))
    s1 = jnp.tile(s1_ref[...], (1, reps))
    s2 = jnp.tile(s2_ref[...], (1, reps))
    seg = seg_ref[...]
    half = ROT_DIM // 2

    def norm_rope(t, gain):
        ms = _dot_x2(t * t, seg) * (1.0 / DA_HEAD_DIM)
        tn = t * lax.rsqrt(ms + NORM_EPS) * gain
        up = pltpu.roll(tn, DA_WIDTH - half, axis=1)
        dn = pltpu.roll(tn, half, axis=1)
        return tn * c + up * s1 + dn * s2

    q = norm_rope(a_ref[:, 0:DA_WIDTH], qg_ref[...])
    k = norm_rope(a_ref[:, DA_WIDTH:2 * DA_WIDTH], kg_ref[...])
    q_ref[...] = (q * (DA_HEAD_DIM ** -0.5)).astype(BF16)
    k_ref[...] = k.astype(BF16)
    v_ref[...] = a_ref[:, 2 * DA_WIDTH:3 * DA_WIDTH].astype(BF16)


def _attn_prep(cols_a, rope, qg, kg, tm):
    t = cols_a.shape[0]
    c, s1, s2 = rope
    lane_spec = pl.BlockSpec((tm, LANES), lambda i: (i, 0))
    out_spec = pl.BlockSpec((tm, DA_WIDTH), lambda i: (i, 0))
    return pl.pallas_call(
        _attn_prep_body,
        grid=(t // tm,),
        in_specs=[
            pl.BlockSpec((tm, DA_COLS), lambda i: (i, 0)),
            lane_spec, lane_spec, lane_spec,
            _full((1, DA_WIDTH)), _full((1, DA_WIDTH)),
            _full((DA_WIDTH, DA_WIDTH)),
        ],
        out_specs=[out_spec, out_spec, out_spec],
        out_shape=[jax.ShapeDtypeStruct((t, DA_WIDTH), BF16)] * 3,
        compiler_params=_cparams(1),
        name="attn_prep",
    )(cols_a, c, s1, s2, qg, kg, _seg_ones(DA_WIDTH, DA_HEAD_DIM))


def _attn_body(q_ref, k_ref, v_ref, lam_ref, sub_ref, o_ref, m_ref, l_ref, acc_ref, s_ref, p_ref,
               *, tq, tw):
    qi = pl.program_id(2)
    lane = lax.broadcasted_iota(jnp.int32, (tq, DA_V_DIM), 1)
    q2 = []
    for h in range(DA_HEAD_GROUP):
        q = q_ref[:, h * DA_V_DIM:(h + 1) * DA_V_DIM]
        zero = jnp.zeros_like(q)
        q2.append(jnp.concatenate([jnp.where(lane < DA_HEAD_DIM, q, zero),
                                   jnp.where(lane >= DA_HEAD_DIM, q, zero)], axis=0))

    m_ref[...] = jnp.full(m_ref.shape, -jnp.inf, F32)
    l_ref[...] = jnp.zeros(l_ref.shape, F32)
    acc_ref[...] = jnp.zeros(acc_ref.shape, F32)

    def span(c0, masked):
        c0 = pl.multiple_of(c0, tw)
        if masked:
            row_chunk = (lax.broadcasted_iota(jnp.int32, (tq, tw), 0) + tq) // CHUNK
            col_chunk = (lax.broadcasted_iota(jnp.int32, (tq, tw), 1) + (c0 - qi * tq + tq)) // CHUNK
            visible = col_chunk <= row_chunk
        for h in range(DA_HEAD_GROUP):
            k = k_ref[pl.ds(c0, tw), h * DA_V_DIM:(h + 1) * DA_V_DIM]
            s2 = lax.dot_general(q2[h], k, (((1,), (1,)), ((), ())), preferred_element_type=F32)
            for m in range(2):
                s = s2[m * tq:(m + 1) * tq]
                s_ref[2 * h + m] = jnp.where(visible, s, -jnp.inf) if masked else s
        for hm in range(2 * DA_HEAD_GROUP):
            s = s_ref[hm]
            m_old = m_ref[hm]
            m_new = jnp.maximum(m_old, jnp.max(s, axis=1, keepdims=True))
            alpha = jnp.exp(m_old - m_new)
            p = jnp.exp(s - jnp.tile(m_new, (1, tw // LANES)))
            p_ref[hm] = p.astype(BF16)
            l_ref[hm] = alpha * l_ref[hm] + jnp.sum(p, axis=1, keepdims=True)
            acc_ref[hm] = alpha * acc_ref[hm]
            m_ref[hm] = m_new
        for h in range(DA_HEAD_GROUP):
            v = v_ref[pl.ds(c0, tw), h * DA_V_DIM:(h + 1) * DA_V_DIM]
            pv = _dot(jnp.concatenate([p_ref[2 * h], p_ref[2 * h + 1]], axis=0), v)
            for m in range(2):
                acc_ref[2 * h + m] += pv[m * tq:(m + 1) * tq]

    def full_span(c, carry):
        span(c * tw, False)
        return carry

    n_full = qi // (tw // tq)
    lax.fori_loop(0, n_full, full_span, 0)
    span(n_full * tw, True)

    lam = lam_ref[...]
    lam_full = (jnp.exp(jnp.sum(lam[0:1] * lam[1:2], axis=1, keepdims=True))
                - jnp.exp(jnp.sum(lam[2:3] * lam[3:4], axis=1, keepdims=True))
                + lam[4:5, 0:1])
    lam_init = lam[4:5, 0:1]
    for h in range(DA_HEAD_GROUP):
        o = acc_ref[2 * h] / l_ref[2 * h] - lam_full * (acc_ref[2 * h + 1] / l_ref[2 * h + 1])
        ms = jnp.mean(o * o, axis=-1, keepdims=True)
        o_ref[:, h * DA_V_DIM:(h + 1) * DA_V_DIM] = (
            o * lax.rsqrt(ms + NORM_EPS) * sub_ref[...] * (1.0 - lam_init))


def _attention(q, k, v, lam_pad, subln, tq):
    bsz, seq, _ = q.shape
    tw = min(2 * tq, seq)
    gw = DA_HEAD_GROUP * DA_V_DIM
    n_maps = 2 * DA_HEAD_GROUP
    kv_spec = pl.BlockSpec((None, seq, gw), lambda b, h, i: (b, 0, h))
    return pl.pallas_call(
        functools.partial(_attn_body, tq=tq, tw=tw),
        grid=(bsz, DA_HEADS // DA_HEAD_GROUP, seq // tq),
        in_specs=[
            pl.BlockSpec((None, tq, gw), lambda b, h, i: (b, i, h)),
            kv_spec, kv_spec,
            _full((SUBLANES, LANES)),
            _full((1, DA_V_DIM)),
        ],
        out_specs=pl.BlockSpec((None, tq, gw), lambda b, h, i: (b, i, h)),
        out_shape=jax.ShapeDtypeStruct((bsz, seq, DA_WIDTH), F32),
        scratch_shapes=[
            pltpu.VMEM((n_maps, tq, LANES), F32),
            pltpu.VMEM((n_maps, tq, LANES), F32),
            pltpu.VMEM((n_maps, tq, DA_V_DIM), F32),
            pltpu.VMEM((n_maps, tq, tw), F32),
            pltpu.VMEM((n_maps, tq, tw), BF16),
        ],
        compiler_params=_cparams(3),
        name="diff_attention",
    )(q, k, v, lam_pad, subln)


def _s5_body(u_ref, bhi_ref, blo_ref, are_ref, aim_ref, chi_ref, clo_ref, d_ref,
             whi_ref, wlo_ref, bg_ref, g_ref, o_ref, st_ref, x_ref, *, tt, bsz):
    ns = SSM_GROUPS * SSM_STATE

    @pl.when(pl.program_id(0) == 0)
    def _():
        st_ref[...] = jnp.zeros(st_ref.shape, F32)

    u = u_ref[...]
    x_ref[...] = _dot_x3(u, bhi_ref[...], blo_ref[...])
    a_re = jnp.broadcast_to(are_ref[...], (bsz, ns))
    a_im = jnp.broadcast_to(aim_ref[...], (bsz, ns))

    def step(t, carry):
        xr, xi = carry
        r = pl.multiple_of(t * bsz, bsz)
        nr = a_re * xr - a_im * xi + x_ref[pl.ds(r, bsz), 0:ns]
        ni = a_re * xi + a_im * xr + x_ref[pl.ds(r, bsz), ns:2 * ns]
        x_ref[pl.ds(r, bsz), 0:ns] = nr
        x_ref[pl.ds(r, bsz), ns:2 * ns] = ni
        return nr, ni

    xr, xi = lax.fori_loop(0, tt, step, (st_ref[:, 0:ns], st_ref[:, ns:2 * ns]))
    st_ref[:, 0:ns] = xr
    st_ref[:, ns:2 * ns] = xi

    y = _dot_x3(x_ref[...], chi_ref[...], clo_ref[...]) + d_ref[...] * u
    yg = _gelu(y)
    z = _dot_x3(yg, whi_ref[...], wlo_ref[...]) + bg_ref[...]
    out = yg * _sigmoid(z)
    ms = jnp.mean(out * out, axis=-1, keepdims=True)
    o_ref[...] = out * lax.rsqrt(ms + NORM_EPS) * g_ref[...]


def _s5(u, ops, tt, bsz):
    rows = u.shape[0]
    ns2 = 2 * SSM_GROUPS * SSM_STATE
    blk = tt * bsz
    io_spec = pl.BlockSpec((blk, SSM_WIDTH), lambda i: (i, 0))
    return pl.pallas_call(
        functools.partial(_s5_body, tt=tt, bsz=bsz),
        grid=(rows // blk,),
        in_specs=[
            io_spec,
            _full((SSM_WIDTH, ns2)), _full((SSM_WIDTH, ns2)),
            _full((1, ns2 // 2)), _full((1, ns2 // 2)),
            _full((ns2, SSM_WIDTH)), _full((ns2, SSM_WIDTH)),
            _full((1, SSM_WIDTH)),
            _full((SSM_WIDTH, SSM_WIDTH)), _full((SSM_WIDTH, SSM_WIDTH)),
            _full((1, SSM_WIDTH)), _full((1, SSM_WIDTH)),
        ],
        out_specs=io_spec,
        out_shape=jax.ShapeDtypeStruct((rows, SSM_WIDTH), F32),
        scratch_shapes=[
            pltpu.VMEM((bsz, ns2), F32),
            pltpu.VMEM((blk, ns2), F32),
        ],
        compiler_params=_cparams(1),
        name="s5",
    )(u, *ops)


def _s5_operands(lam_re, lam_im, b_re, b_im, c_re, c_im, d_skip, log_dt, w_glu, b_glu, out_norm):
    lr = jnp.minimum(lam_re, -1e-4)
    li = lam_im
    step = jnp.exp(log_dt)[:, None]
    mag = jnp.exp(lr * step)
    abar_re = mag * jnp.cos(li * step)
    abar_im = mag * jnp.sin(li * step)
    den = lr * lr + li * li
    coef_re = ((abar_re - 1.0) * lr + abar_im * li) / den
    coef_im = (abar_im * lr - (abar_re - 1.0) * li) / den
    bbar_re = coef_re[..., None] * b_re - coef_im[..., None] * b_im
    bbar_im = coef_re[..., None] * b_im + coef_im[..., None] * b_re
    eye = jnp.eye(SSM_GROUPS, dtype=F32)
    ns = SSM_GROUPS * SSM_STATE

    def in_mat(bb):
        return jnp.einsum('gph,gk->ghkp', bb, eye).reshape(SSM_WIDTH, ns)

    def out_mat(cc):
        return jnp.einsum('ghp,gk->kpgh', cc, eye).reshape(ns, SSM_WIDTH)

    bmat = jnp.concatenate([in_mat(bbar_re), in_mat(bbar_im)], axis=1)
    cmat = jnp.concatenate([out_mat(c_re), -out_mat(c_im)], axis=0)
    return (*_wsplit(bmat), abar_re.reshape(1, ns), abar_im.reshape(1, ns), *_wsplit(cmat),
            d_skip.reshape(1, SSM_WIDTH), *_wsplit(w_glu), b_glu.reshape(1, SSM_WIDTH),
            out_norm.reshape(1, SSM_WIDTH))


def _rwkv_body(c_ref, mu_ref, wdh_ref, wdl_ref, wah_ref, wal_ref, wgh_ref, wgl_ref,
               w0_ref, a0_ref, kk_ref, ka_ref, rk_ref, lnw_ref, lnb_ref,
               seg256_ref, seg128_ref, eye_ref,
               o_ref, prev_ref, st_ref, r_s, w_s, k_s, v_s, a_s, b_s, y_s, *, tt, bsz):
    n_pairs = RW_WIDTH // LANES
    n_inst = n_pairs * bsz
    hd = RW_HEAD_DIM

    @pl.when(pl.program_id(0) == 0)
    def _():
        prev_ref[...] = jnp.zeros(prev_ref.shape, F32)
        st_ref[...] = jnp.zeros(st_ref.shape, F32)

    c = c_ref[...]
    rows = c.shape[0]
    prev = jnp.concatenate([prev_ref[...], c[0:rows - bsz]], axis=0)
    prev_ref[...] = c[rows - bsz:rows]
    cs = c + (prev - c) * mu_ref[...]
    r = cs[:, 0:RW_WIDTH]
    k = cs[:, RW_WIDTH:2 * RW_WIDTH]
    v = cs[:, 2 * RW_WIDTH:3 * RW_WIDTH]
    x3 = cs[:, 3 * RW_WIDTH:RW_COLS]

    seg256 = seg256_ref[...]
    wdec = w0_ref[...] + _dot_x3(jnp.tanh(x3), wdh_ref[...], wdl_ref[...])
    z = -wdec
    softplus = jnp.maximum(z, 0.0) + jnp.log(1.0 + jnp.exp(-jnp.abs(z)))
    decay = jnp.exp(-jnp.exp(-softplus - 0.5))
    a = _sigmoid(a0_ref[...] + _dot_x3(x3, wah_ref[...], wal_ref[...]))
    g = _dot_x3(_sigmoid(x3), wgh_ref[...], wgl_ref[...])
    kk = k * kk_ref[...]
    kk = kk / jnp.maximum(jnp.sqrt(_dot_x2(kk * kk, seg256)), 1e-12)
    k2 = k * (1.0 + (a - 1.0) * ka_ref[...])
    r_s[...] = r
    w_s[...] = decay
    k_s[...] = k2
    v_s[...] = v
    a_s[...] = -kk
    b_s[...] = kk * a

    seg128x2 = seg128_ref[...]
    seg128 = seg128x2[0:LANES]
    eye = eye_ref[...]
    sub = lax.broadcasted_iota(jnp.int32, (bsz, LANES), 0)

    def vec_of(block, inst):
        p, b = divmod(inst, bsz)
        return block[b:b + 1, p * LANES:(p + 1) * LANES]

    def read_out(states, r_blk, dst_row):
        lhs_y = jnp.concatenate([states[i] * vec_of(r_blk, i) for i in range(n_inst)], axis=0)
        ycol = _dot(lhs_y.astype(BF16), seg128)
        for p in range(n_pairs):
            yblk = jnp.zeros((bsz, LANES), F32)
            for b in range(bsz):
                i = p * bsz + b
                yrow = jnp.sum(eye * ycol[i * hd:(i + 1) * hd], axis=0, keepdims=True)
                yblk = jnp.where(sub == b, yrow, yblk)
            y_s[pl.ds(dst_row, bsz), p * LANES:(p + 1) * LANES] = yblk

    def step(t, carry):
        row0 = pl.multiple_of(t * bsz, bsz)
        blk = [ref[pl.ds(row0, bsz), :] for ref in (w_s, k_s, v_s, a_s, b_s)]
        vec = lambda q, inst: vec_of(blk[q], inst)
        states = [st_ref[i * hd:(i + 1) * hd, :] for i in range(n_inst)]
        prev_row = pl.multiple_of(jnp.maximum(row0 - bsz, 0), bsz)
        read_out(states, r_s[pl.ds(prev_row, bsz), :], row0)
        lhs_a = jnp.concatenate([states[i] * vec(3, i) for i in range(n_inst)], axis=0)
        lhs_v = jnp.concatenate([eye * vec(2, i) for i in range(n_inst)], axis=0)
        sa = _dot_x2k(lhs_a, seg128x2)
        vcol = _dot_x2k(lhs_v, seg128x2)
        for i in range(n_inst):
            sl = slice(i * hd, (i + 1) * hd)
            st_ref[sl, :] = states[i] * vec(0, i) + sa[sl] * vec(4, i) + vcol[sl] * vec(1, i)
        return carry

    lax.fori_loop(0, tt, step, 0)
    last = (tt - 1) * bsz
    read_out([st_ref[i * hd:(i + 1) * hd, :] for i in range(n_inst)], r_s[last:last + bsz, :], tt * bsz)

    y = y_s[bsz:(tt + 1) * bsz, :]
    inv = 1.0 / hd
    mean = _dot_x2(y, seg256) * inv
    dv = y - mean
    var = _dot_x2(dv * dv, seg256) * inv
    yn = dv * lax.rsqrt(var + RW_LN_EPS) * lnw_ref[...] + lnb_ref[...]
    bonus = _dot_x2(r * k2 * rk_ref[...], seg256) * v
    o_ref[...] = (yn + bonus) * g


def _rwkv(c, ops, tt, bsz):
    rows = c.shape[0]
    blk = tt * bsz
    vec_spec = _full((1, RW_WIDTH))
    pad_spec = _full((LANES, RW_WIDTH))
    n_inst = (RW_WIDTH // LANES) * bsz
    return pl.pallas_call(
        functools.partial(_rwkv_body, tt=tt, bsz=bsz),
        grid=(rows // blk,),
        in_specs=[
            pl.BlockSpec((blk, RW_COLS), lambda i: (i, 0)),
            _full((1, RW_COLS)),
            pad_spec, pad_spec, pad_spec, pad_spec, pad_spec, pad_spec,
            vec_spec, vec_spec, vec_spec, vec_spec, vec_spec, vec_spec, vec_spec,
            _full((RW_WIDTH, RW_WIDTH)), _full((2 * LANES, LANES)), _full((RW_HEAD_DIM, LANES)),
        ],
        out_specs=pl.BlockSpec((blk, RW_WIDTH), lambda i: (i, 0)),
        out_shape=jax.ShapeDtypeStruct((rows, RW_WIDTH), F32),
        scratch_shapes=[
            pltpu.VMEM((bsz, RW_COLS), F32),
            pltpu.VMEM((n_inst * RW_HEAD_DIM, LANES), F32),
        ] + [pltpu.VMEM((blk, RW_WIDTH), F32)] * 6 + [pltpu.VMEM((blk + bsz, RW_WIDTH), F32)],
        compiler_params=_cparams(1),
        name="rwkv7",
    )(c, *ops)


def _rwkv_operands(mu, decay_up, decay_w0, aaa_up, aaa_a0, gate_up, k_k, k_a, r_k, ln_w, ln_b):
    def pad(w, off):
        return _wsplit(jnp.zeros((LANES, RW_WIDTH), F32).at[off:off + w.shape[0]].set(w))

    row = lambda t: t.reshape(1, RW_WIDTH)
    lane = jnp.arange(LANES)
    eye = (lane[None, :] % RW_HEAD_DIM == jnp.arange(RW_HEAD_DIM)[:, None]).astype(F32)
    return (mu.reshape(1, RW_COLS),
            *pad(decay_up, 0), *pad(aaa_up, RW_DECAY_RANK),
            *pad(gate_up, RW_DECAY_RANK + RW_AAA_RANK),
            row(decay_w0), row(aaa_a0), row(k_k), row(k_a), row(r_k), row(ln_w), row(ln_b),
            _seg_ones(RW_WIDTH, RW_HEAD_DIM),
            jnp.concatenate([_seg_ones(LANES, RW_HEAD_DIM)] * 2, axis=0), eye)


def _outproj_body(x_ref, ya_ref, yb_ref, yc_ref, wa_ref, wb_ref, wc_ref, g_ref, wq_ref,
                  keys_ref, xn_ref, h_ref, st_ref):
    xn = (x_ref[...]
          + _dot(ya_ref[...].astype(BF16), wa_ref[...])
          + _dot(yb_ref[...].astype(BF16), wb_ref[...])
          + _dot(yc_ref[...].astype(BF16), wc_ref[...]))
    xn_ref[...] = xn
    ms = jnp.mean(xn * xn, axis=-1, keepdims=True)
    h = xn * lax.rsqrt(ms + NORM_EPS) * g_ref[...]
    h_ref[...] = h
    q = _dot(h.astype(BF16), wq_ref[...]).astype(BF16)
    for hm in range(2 * PEER_HEADS):
        st_ref[hm * N_KEYS:(hm + 1) * N_KEYS, :] = lax.dot_general(
            keys_ref[hm], q[:, hm * PEER_KEY_DIM:(hm + 1) * PEER_KEY_DIM],
            (((1,), (1,)), ((), ())), preferred_element_type=F32)


def _outproj(x, ya, yb, yc, wa, wb, wc, g, wq, keys, tl):
    bsz, seq, _ = x.shape
    nl = seq // tl
    n_rows = 2 * PEER_HEADS * N_KEYS
    tok_spec = pl.BlockSpec((None, tl, D_MODEL), lambda b, i: (b, i, 0))
    tm_spec = pl.BlockSpec((tl, SSM_WIDTH), lambda b, i: (i, b))
    return pl.pallas_call(
        _outproj_body,
        grid=(bsz, nl),
        in_specs=[
            tok_spec,
            pl.BlockSpec((None, tl, DA_WIDTH), lambda b, i: (b, i, 0)),
            tm_spec, tm_spec,
            _full((DA_WIDTH, D_MODEL)), _full((SSM_WIDTH, D_MODEL)), _full((RW_WIDTH, D_MODEL)),
            _full((1, D_MODEL)),
            _full((D_MODEL, n_rows)),
            _full((2 * PEER_HEADS, N_KEYS, PEER_KEY_DIM)),
        ],
        out_specs=[
            tok_spec, tok_spec,
            pl.BlockSpec((n_rows, tl), lambda b, i: (0, b * nl + i)),
        ],
        out_shape=[
            jax.ShapeDtypeStruct((bsz, seq, D_MODEL), F32),
            jax.ShapeDtypeStruct((bsz, seq, D_MODEL), F32),
            jax.ShapeDtypeStruct((n_rows, bsz * seq), F32),
        ],
        compiler_params=_cparams(2),
        name="outproj_peer_scores",
    )(x, ya, yb, yc, wa, wb, wc, g, wq, keys)


def _topk_rows(s, k, payload=None):
    n, width = s.shape
    rows = lax.broadcasted_iota(jnp.int32, s.shape, 0).astype(F32)
    out_rows = lax.broadcasted_iota(jnp.int32, (k, width), 0)
    vals = jnp.zeros((k, width), F32)
    picks = jnp.zeros((k, width), F32)
    for j in range(k):
        m = jnp.max(s, axis=0, keepdims=True)
        key = jnp.where(s == m, rows, float(n))
        pos = jnp.min(key, axis=0, keepdims=True)
        hit = key == pos
        pick = pos if payload is None else jnp.max(jnp.where(hit, payload, -1.0), axis=0, keepdims=True)
        vals = jnp.where(out_rows == j, m, vals)
        picks = jnp.where(out_rows == j, pick, picks)
        s = jnp.where(hit, -jnp.inf, s)
    return vals, picks


def _candidates(t1, t2, combine, fill):
    k = PEER_TOPK
    sub = lax.broadcasted_iota(jnp.int32, (SUBLANES, t1.shape[1]), 0)
    blocks = [combine(t1[0:1], t2)]
    for a in range(1, SUBLANES):
        n = k // (a + 1)
        blk = combine(t1[a:a + 1], t2[0:SUBLANES])
        blocks.append(blk if n >= SUBLANES else jnp.where(sub < n, blk, fill))
    blocks.append(combine(t1[SUBLANES:k], t2[0:1]))
    return jnp.concatenate(blocks, axis=0)


def _topk_body(st_ref, idx_ref, gate_ref):
    idx_rows, gate_rows = [], []
    for h in range(PEER_HEADS):
        v1, i1 = _topk_rows(st_ref[(2 * h) * N_KEYS:(2 * h + 1) * N_KEYS, :], PEER_TOPK)
        v2, i2 = _topk_rows(st_ref[(2 * h + 1) * N_KEYS:(2 * h + 2) * N_KEYS, :], PEER_TOPK)
        cand_s = _candidates(v1, v2, lambda x, y: x + y, -jnp.inf)
        cand_e = _candidates(i1, i2, lambda x, y: x * N_KEYS + y, -1.0)
        best, expert = _topk_rows(cand_s, PEER_TOPK, payload=cand_e)
        e = jnp.exp(best - best[0:1])
        gate_rows.append(e / jnp.sum(e, axis=0, keepdims=True))
        idx_rows.append((expert * PEER_ROW_SUBLANES).astype(jnp.int32))
    gate_ref[...] = jnp.concatenate(gate_rows, axis=0)
    idx_ref[...] = jnp.concatenate(idx_rows, axis=0).T


def _topk(scores_t, tb):
    t = scores_t.shape[1]
    return pl.pallas_call(
        _topk_body,
        grid=(t // tb,),
        in_specs=[pl.BlockSpec((scores_t.shape[0], tb), lambda i: (0, i))],
        out_specs=[
            pl.BlockSpec((tb, PEER_SLOTS), lambda i: (i, 0)),
            pl.BlockSpec((PEER_SLOTS, tb), lambda i: (0, i)),
        ],
        out_shape=[
            jax.ShapeDtypeStruct((t, PEER_SLOTS), jnp.int32),
            jax.ShapeDtypeStruct((PEER_SLOTS, t), F32),
        ],
        compiler_params=_cparams(1),
        name="peer_topk",
    )(scores_t)


def _pack_body(t_ref, o_ref):
    n = t_ref.shape[0]
    bits = lax.bitcast_convert_type(t_ref[...].astype(BF16).astype(F32), jnp.uint32)
    words = (bits[:, :PEER_ROW_WORDS] >> 16) | (bits[:, PEER_ROW_WORDS:] & jnp.uint32(0xFFFF0000))
    for r in range(PEER_ROW_SUBLANES):
        o_ref[pl.ds(r, n, stride=PEER_ROW_SUBLANES), :] = words[:, r * LANES:(r + 1) * LANES]


def _pack_table(tab, te=512):
    n = tab.shape[0]
    return pl.pallas_call(
        _pack_body,
        grid=(n // te,),
        in_specs=[pl.BlockSpec((te, D_MODEL), lambda i: (i, 0))],
        out_specs=pl.BlockSpec((te * PEER_ROW_SUBLANES, LANES), lambda i: (i, 0)),
        out_shape=jax.ShapeDtypeStruct((n * PEER_ROW_SUBLANES, LANES), jnp.uint32),
        compiler_params=_cparams(1),
        name="peer_pack_table",
    )(tab)


PEER_TILE_ROWS = PEER_SLOTS * PEER_ROW_SUBLANES
PEER_UNROLL = SUBLANES


def _gather_rows(tab_ref, idx_ref, tt, dst_ref, u):
    rs = PEER_ROW_SUBLANES
    for k in range(PEER_SLOTS):
        off = pl.multiple_of(idx_ref[tt, k], rs)
        dst_ref[pl.ds(u * PEER_TILE_ROWS + k * rs, rs), :] = tab_ref[pl.ds(off, rs), :]


def _row_plane(rows_ref, u, r):
    words = rows_ref[pl.ds(u * PEER_TILE_ROWS + r, PEER_SLOTS, stride=PEER_ROW_SUBLANES), :]
    lo = lax.bitcast_convert_type(words << 16, F32)
    hi = lax.bitcast_convert_type(words & jnp.uint32(0xFFFF0000), F32)
    return lo, hi


def _table_spec(tab):
    return pl.BlockSpec(tab.shape, lambda i: (0, 0), pipeline_mode=pl.Buffered(1))


def _peer_act_body(idx_ref, h_ref, gate_ref, tab_ref, coef_ref, rows_ref, act_ref, *, tb):
    lane = lax.broadcasted_iota(jnp.int32, (PEER_SLOTS, tb), 1)
    half = D_MODEL // 2
    act_ref[...] = jnp.zeros(act_ref.shape, F32)

    def group(g, carry):
        row0 = pl.multiple_of(g * PEER_UNROLL, PEER_UNROLL)
        h8 = h_ref[pl.ds(row0, PEER_UNROLL), :]
        for u in range(PEER_UNROLL):
            _gather_rows(tab_ref, idx_ref, row0 + u, rows_ref, u)
            part = jnp.zeros((PEER_SLOTS, LANES), F32)
            for r in range(PEER_ROW_SUBLANES):
                lo, hi = _row_plane(rows_ref, u, r)
                part = (part + lo * h8[u:u + 1, r * LANES:(r + 1) * LANES]
                        + hi * h8[u:u + 1, half + r * LANES:half + (r + 1) * LANES])
            act = jnp.sum(part, axis=1, keepdims=True)
            act_ref[...] = jnp.where(lane == row0 + u, act, act_ref[...])
        return carry

    lax.fori_loop(0, tb // PEER_UNROLL, group, 0)
    coef_ref[...] = _gelu(act_ref[...]) * gate_ref[...]


def _peer_act(idx, h, gate_t, tab, tb):
    t = idx.shape[0]
    slot_major = pl.BlockSpec((PEER_SLOTS, tb), lambda i: (0, i))
    return pl.pallas_call(
        functools.partial(_peer_act_body, tb=tb),
        grid=(t // tb,),
        in_specs=[
            pl.BlockSpec((tb, PEER_SLOTS), lambda i: (i, 0), memory_space=pltpu.SMEM),
            pl.BlockSpec((tb, D_MODEL), lambda i: (i, 0)),
            slot_major,
            _table_spec(tab),
        ],
        out_specs=slot_major,
        out_shape=jax.ShapeDtypeStruct((PEER_SLOTS, t), F32),
        scratch_shapes=[
            pltpu.VMEM((PEER_UNROLL * PEER_TILE_ROWS, LANES), jnp.uint32),
            pltpu.VMEM((PEER_SLOTS, tb), F32),
        ],
        compiler_params=_cparams(1),
        name="peer_act",
    )(idx, h, gate_t, tab)


def _peer_out_body(idx_ref, coef_ref, x_ref, ones_ref, tab_ref, o_ref, rows_ref, *, tb):
    lane = lax.broadcasted_iota(jnp.int32, (PEER_SLOTS, tb), 1)
    sub = lax.broadcasted_iota(jnp.int32, (PEER_UNROLL, LANES), 0)
    ones = ones_ref[...]
    n_planes = PEER_ROW_SUBLANES

    def group(g, carry):
        row0 = pl.multiple_of(g * PEER_UNROLL, PEER_UNROLL)
        planes = [jnp.zeros((PEER_UNROLL, LANES), F32) for _ in range(2 * n_planes)]
        for u in range(PEER_UNROLL):
            _gather_rows(tab_ref, idx_ref, row0 + u, rows_ref, u)
            ccol = _dot_x2k(jnp.where(lane == row0 + u, coef_ref[...], 0.0), ones)
            for r in range(n_planes):
                lo, hi = _row_plane(rows_ref, u, r)
                planes[r] = jnp.where(sub == u, jnp.sum(lo * ccol, axis=0, keepdims=True), planes[r])
                planes[n_planes + r] = jnp.where(
                    sub == u, jnp.sum(hi * ccol, axis=0, keepdims=True), planes[n_planes + r])
        o_ref[pl.ds(row0, PEER_UNROLL), :] = (x_ref[pl.ds(row0, PEER_UNROLL), :]
                                              + jnp.concatenate(planes, axis=1))
        return carry

    lax.fori_loop(0, tb // PEER_UNROLL, group, 0)


def _peer_out(idx, coef_t, x, tab, tb):
    t = idx.shape[0]
    tok_spec = pl.BlockSpec((tb, D_MODEL), lambda i: (i, 0))
    ones = jnp.ones((2 * tb, LANES), BF16)
    return pl.pallas_call(
        functools.partial(_peer_out_body, tb=tb),
        grid=(t // tb,),
        in_specs=[
            pl.BlockSpec((tb, PEER_SLOTS), lambda i: (i, 0), memory_space=pltpu.SMEM),
            pl.BlockSpec((PEER_SLOTS, tb), lambda i: (0, i)),
            tok_spec,
            _full(ones.shape),
            _table_spec(tab),
        ],
        out_specs=tok_spec,
        out_shape=jax.ShapeDtypeStruct(x.shape, F32),
        scratch_shapes=[pltpu.VMEM((PEER_UNROLL * PEER_TILE_ROWS, LANES), jnp.uint32)],
        compiler_params=_cparams(1),
        name="peer_out",
    )(idx, coef_t, x, ones, tab)


def _tiles(bsz, seq):
    tl = min(seq, 512)
    tq = min(seq, 256)
    tt_s5 = min(seq, 128)
    tt_rw = min(seq, 64)
    tb = 128
    return tl, tq, tt_s5, tt_rw, tb


def kernel(x, positions, norm_mix, w_in, da_q_norm, da_k_norm, da_lambda, da_subln, ssm_lambda_re, ssm_lambda_im, ssm_b_re, ssm_b_im, ssm_c_re, ssm_c_im, ssm_d, ssm_log_dt, ssm_w_glu, ssm_b_glu, ssm_out_norm, rw_mu, rw_decay_up, rw_decay_w0, rw_aaa_up, rw_aaa_a0, rw_gate_up, rw_k_k, rw_k_a, rw_r_k, rw_ln_w, rw_ln_b, w_out, norm_ffn, peer_w_q, peer_sub_keys, peer_u, peer_v):
    bsz, seq, _ = x.shape
    depth = w_in.shape[0]
    t = bsz * seq
    tl, tq, tt_s5, tt_rw, tb = _tiles(bsz, seq)
    rope = _rope_tables(positions, tl)

    for l in range(depth):
        w = w_in[l].astype(BF16)
        cols_a, cols_b, cols_c = _inproj(
            x, norm_mix[l].reshape(1, D_MODEL), w[:, :DA_COLS],
            w[:, DA_COLS:DA_COLS + SSM_WIDTH], w[:, DA_COLS + SSM_WIDTH:], tl)

        q, k, v = _attn_prep(cols_a.reshape(t, DA_COLS), rope,
                             da_q_norm[l].reshape(1, DA_WIDTH), da_k_norm[l].reshape(1, DA_WIDTH), tl)
        lam_init = 0.8 - 0.6 * math.exp(-0.3 * l)
        lam_pad = (jnp.zeros((SUBLANES, LANES), F32)
                   .at[0:4, 0:DA_HEAD_DIM].set(da_lambda[l]).at[4, :].set(lam_init))
        y_a = _attention(q.reshape(bsz, seq, DA_WIDTH), k.reshape(bsz, seq, DA_WIDTH),
                         v.reshape(bsz, seq, DA_WIDTH), lam_pad, da_subln[l].reshape(1, DA_V_DIM), tq)

        s5_ops = _s5_operands(ssm_lambda_re[l], ssm_lambda_im[l], ssm_b_re[l], ssm_b_im[l],
                              ssm_c_re[l], ssm_c_im[l], ssm_d[l], ssm_log_dt[l], ssm_w_glu[l],
                              ssm_b_glu[l], ssm_out_norm[l])
        y_b = _s5(cols_b.reshape(t, SSM_WIDTH), s5_ops, tt_s5, bsz)

        rw_ops = _rwkv_operands(rw_mu[l], rw_decay_up[l], rw_decay_w0[l], rw_aaa_up[l], rw_aaa_a0[l],
                                rw_gate_up[l], rw_k_k[l], rw_k_a[l], rw_r_k[l], rw_ln_w[l], rw_ln_b[l])
        y_c = _rwkv(cols_c.reshape(t, RW_COLS), rw_ops, tt_rw, bsz)

        wo = w_out[l].astype(BF16)
        keys = peer_sub_keys[l].reshape(2 * PEER_HEADS, N_KEYS, PEER_KEY_DIM).astype(BF16)
        x_mid, h_ffn, scores_t = _outproj(
            x, y_a, y_b.reshape(seq, bsz * SSM_WIDTH), y_c.reshape(seq, bsz * RW_WIDTH),
            wo[:DA_WIDTH], wo[DA_WIDTH:DA_WIDTH + SSM_WIDTH], wo[DA_WIDTH + SSM_WIDTH:],
            norm_ffn[l].reshape(1, D_MODEL), peer_w_q[l].astype(BF16), keys, tl)

        idx, gate_t = _topk(scores_t, tb)
        coef_t = _peer_act(idx, h_ffn.reshape(t, D_MODEL), gate_t, _pack_table(peer_u[l]), tb)
        x = _peer_out(idx, coef_t, x_mid.reshape(t, D_MODEL),
                      _pack_table(peer_v[l]), tb).reshape(bsz, seq, D_MODEL)
    return x
```

```python
import functools
import math

import jax
import jax.numpy as jnp
from jax import lax
from jax.experimental import pallas as pl
from jax.experimental.pallas import tpu as pltpu

D_MODEL = 1024
CHUNK = 64
NORM_EPS = 1e-6

DA_HEADS = 4
DA_HEAD_DIM = 64
DA_V_DIM = 128
DA_WIDTH = 512
DA_HEAD_GROUP = 4
ROT_DIM = 16
ROPE_THETA = 500000.0

SSM_GROUP = 16
SSM_GROUPS = 16
SSM_WIDTH = 256
SSM_STATE = 64

RW_HEADS = 4
RW_HEAD_DIM = 64
RW_WIDTH = 256
RW_DECAY_RANK = 32
RW_AAA_RANK = 32
RW_GATE_RANK = 64
RW_LN_EPS = 64e-5
RW_COLS = 896

DA_COLS = 1536

PEER_HEADS = 8
N_KEYS = 128
PEER_KEY_DIM = 128
PEER_TOPK = 16
PEER_SLOTS = PEER_HEADS * PEER_TOPK
PEER_ROW_WORDS = D_MODEL // 2
PEER_ROW_SUBLANES = PEER_ROW_WORDS // 128

LANES = 128
SUBLANES = 8
VMEM_LIMIT_BYTES = 56 * 1024 * 1024

F32 = jnp.float32
BF16 = jnp.bfloat16


def _cparams(n_axes, vmem=VMEM_LIMIT_BYTES):
    return pltpu.CompilerParams(
        dimension_semantics=("arbitrary",) * n_axes, vmem_limit_bytes=vmem)


def _dot(a, b):
    return jnp.dot(a, b, preferred_element_type=F32)


def _split(x):
    hi = x.astype(BF16)
    lo = (x - hi.astype(F32)).astype(BF16)
    return hi, lo


def _dot_x2(x, w):
    hi, lo = _split(x)
    return _dot(hi, w) + _dot(lo, w)


def _dot_x2k(x, w_stacked):
    hi, lo = _split(x)
    return _dot(jnp.concatenate([hi, lo], axis=1), w_stacked)


def _dot_x3(x, w_hi, w_lo):
    hi, lo = _split(x)
    return _dot(hi, w_hi) + _dot(lo, w_hi) + _dot(hi, w_lo)


def _wsplit(w):
    hi = w.astype(BF16)
    lo = (w - hi.astype(F32)).astype(BF16)
    return hi, lo


def _sigmoid(x):
    return 1.0 / (1.0 + jnp.exp(-x))


def _gelu(x):
    return 0.5 * x * (1.0 + lax.erf(x * (1.0 / math.sqrt(2.0))))


def _seg_ones(n, seg):
    r = jnp.arange(n) // seg
    return (r[:, None] == r[None, :]).astype(BF16)


def _full(shape):
    nd = len(shape)
    return pl.BlockSpec(shape, lambda *_: (0,) * nd)


def _inproj_body(x_ref, g_ref, wa_ref, wb_ref, wc_ref, a_ref, b_ref, c_ref):
    x = x_ref[...]
    ms = jnp.mean(x * x, axis=-1, keepdims=True)
    h = (x * lax.rsqrt(ms + NORM_EPS) * g_ref[...]).astype(BF16)
    a_ref[...] = _dot(h, wa_ref[...])
    b_ref[...] = _dot(h, wb_ref[...])
    c_ref[...] = _dot(h, wc_ref[...])


def _inproj(x, g, wa, wb, wc, tl):
    bsz, seq, _ = x.shape
    return pl.pallas_call(
        _inproj_body,
        grid=(bsz, seq // tl),
        in_specs=[
            pl.BlockSpec((None, tl, D_MODEL), lambda b, i: (b, i, 0)),
            _full((1, D_MODEL)),
            _full((D_MODEL, DA_COLS)),
            _full((D_MODEL, SSM_WIDTH)),
            _full((D_MODEL, RW_COLS)),
        ],
        out_specs=[
            pl.BlockSpec((None, tl, DA_COLS), lambda b, i: (b, i, 0)),
            pl.BlockSpec((tl, SSM_WIDTH), lambda b, i: (i, b)),
            pl.BlockSpec((tl, RW_COLS), lambda b, i: (i, b)),
        ],
        out_shape=[
            jax.ShapeDtypeStruct((bsz, seq, DA_COLS), F32),
            jax.ShapeDtypeStruct((seq, bsz * SSM_WIDTH), F32),
            jax.ShapeDtypeStruct((seq, bsz * RW_COLS), F32),
        ],
        compiler_params=_cparams(2),
        name="inproj",
    )(x, g, wa, wb, wc)


def _rope_body(pos_ref, invf_ref, c_ref, s1_ref, s2_ref):
    ang = pos_ref[...].astype(F32) * invf_ref[...]
    c, s = jnp.cos(ang), jnp.sin(ang)
    lm = lax.broadcasted_iota(jnp.int32, ang.shape, 1) % DA_HEAD_DIM
    half = ROT_DIM // 2
    c_ref[...] = jnp.where(lm < ROT_DIM, c, 1.0)
    s1_ref[...] = jnp.where(lm < half, -s, 0.0)
    s2_ref[...] = jnp.where((lm >= half) & (lm < ROT_DIM), s, 0.0)


def _rope_tables(positions, tm):
    t = positions.size
    half = ROT_DIM // 2
    inv_freq = ROPE_THETA ** (-jnp.arange(0, ROT_DIM, 2, dtype=F32) / ROT_DIM)
    invf = inv_freq[jnp.arange(LANES) % half][None, :]
    pos = jnp.broadcast_to(positions.reshape(t, 1), (t, LANES))
    spec = pl.BlockSpec((tm, LANES), lambda i: (i, 0))
    return pl.pallas_call(
        _rope_body,
        grid=(t // tm,),
        in_specs=[spec, _full((1, LANES))],
        out_specs=[spec, spec, spec],
        out_shape=[jax.ShapeDtypeStruct((t, LANES), F32)] * 3,
        compiler_params=_cparams(1),
        name="rope_tables",
    )(pos, invf)


def _attn_prep_body(a_ref, c_ref, s1_ref, s2_ref, qg_ref, kg_ref, seg_ref,
                    q_ref, k_ref, v_ref):
    reps = DA_WIDTH // LANES
    c = jnp.tile(c_ref[...], (1, reps))
    s1 = jnp.tile(s1_ref[...], (1, reps))
    s2 = jnp.tile(s2_ref[...], (1, reps))
    seg = seg_ref[...]
    half = ROT_DIM // 2

    def norm_rope(t, gain):
        ms = _dot_x2(t * t, seg) * (1.0 / DA_HEAD_DIM)
        tn = t * lax.rsqrt(ms + NORM_EPS) * gain
        up = pltpu.roll(tn, DA_WIDTH - half, axis=1)
        dn = pltpu.roll(tn, half, axis=1)
        return tn * c + up * s1 + dn * s2

    q = norm_rope(a_ref[:, 0:DA_WIDTH], qg_ref[...])
    k = norm_rope(a_ref[:, DA_WIDTH:2 * DA_WIDTH], kg_ref[...])
    q_ref[...] = (q * (DA_HEAD_DIM ** -0.5)).astype(BF16)
    k_ref[...] = k.astype(BF16)
    v_ref[...] = a_ref[:, 2 * DA_WIDTH:3 * DA_WIDTH].astype(BF16)


def _attn_prep(cols_a, rope, qg, kg, tm):
    t = cols_a.shape[0]
    c, s1, s2 = rope
    lane_spec = pl.BlockSpec((tm, LANES), lambda i: (i, 0))
    out_spec = pl.BlockSpec((tm, DA_WIDTH), lambda i: (i, 0))
    return pl.pallas_call(
        _attn_prep_body,
        grid=(t // tm,),
        in_specs=[
            pl.BlockSpec((tm, DA_COLS), lambda i: (i, 0)),
            lane_spec, lane_spec, lane_spec,
            _full((1, DA_WIDTH)), _full((1, DA_WIDTH)),
            _full((DA_WIDTH, DA_WIDTH)),
        ],
        out_specs=[out_spec, out_spec, out_spec],
        out_shape=[jax.ShapeDtypeStruct((t, DA_WIDTH), BF16)] * 3,
        compiler_params=_cparams(1),
        name="attn_prep",
    )(cols_a, c, s1, s2, qg, kg, _seg_ones(DA_WIDTH, DA_HEAD_DIM))


def _attn_body(q_ref, k_ref, v_ref, lam_ref, sub_ref, o_ref, m_ref, l_ref, acc_ref, s_ref, p_ref,
               *, tq, tw):
    qi = pl.program_id(2)
    lane = lax.broadcasted_iota(jnp.int32, (tq, DA_V_DIM), 1)
    q2 = []
    for h in range(DA_HEAD_GROUP):
        q = q_ref[:, h * DA_V_DIM:(h + 1) * DA_V_DIM]
        zero = jnp.zeros_like(q)
        q2.append(jnp.concatenate([jnp.where(lane < DA_HEAD_DIM, q, zero),
                                   jnp.where(lane >= DA_HEAD_DIM, q, zero)], axis=0))

    m_ref[...] = jnp.full(m_ref.shape, -jnp.inf, F32)
    l_ref[...] = jnp.zeros(l_ref.shape, F32)
    acc_ref[...] = jnp.zeros(acc_ref.shape, F32)

    def span(c0, masked):
        c0 = pl.multiple_of(c0, tw)
        if masked:
            row_chunk = (lax.broadcasted_iota(jnp.int32, (tq, tw), 0) + tq) // CHUNK
            col_chunk = (lax.broadcasted_iota(jnp.int32, (tq, tw), 1) + (c0 - qi * tq + tq)) // CHUNK
            visible = col_chunk <= row_chunk
        for h in range(DA_HEAD_GROUP):
            k = k_ref[pl.ds(c0, tw), h * DA_V_DIM:(h + 1) * DA_V_DIM]
            s2 = lax.dot_general(q2[h], k, (((1,), (1,)), ((), ())), preferred_element_type=F32)
            for m in range(2):
                s = s2[m * tq:(m + 1) * tq]
                s_ref[2 * h + m] = jnp.where(visible, s, -jnp.inf) if masked else s
        for hm in range(2 * DA_HEAD_GROUP):
            s = s_ref[hm]
            m_old = m_ref[hm]
            m_new = jnp.maximum(m_old, jnp.max(s, axis=1, keepdims=True))
            alpha = jnp.exp(m_old - m_new)
            p = jnp.exp(s - jnp.tile(m_new, (1, tw // LANES)))
            p_ref[hm] = p.astype(BF16)
            l_ref[hm] = alpha * l_ref[hm] + jnp.sum(p, axis=1, keepdims=True)
            acc_ref[hm] = alpha * acc_ref[hm]
            m_ref[hm] = m_new
        for h in range(DA_HEAD_GROUP):
            v = v_ref[pl.ds(c0, tw), h * DA_V_DIM:(h + 1) * DA_V_DIM]
            pv = _dot(jnp.concatenate([p_ref[2 * h], p_ref[2 * h + 1]], axis=0), v)
            for m in range(2):
                acc_ref[2 * h + m] += pv[m * tq:(m + 1) * tq]

    def full_span(c, carry):
        span(c * tw, False)
        return carry

    n_full = qi // (tw // tq)
    lax.fori_loop(0, n_full, full_span, 0)
    span(n_full * tw, True)

    lam = lam_ref[...]
    lam_full = (jnp.exp(jnp.sum(lam[0:1] * lam[1:2], axis=1, keepdims=True))
                - jnp.exp(jnp.sum(lam[2:3] * lam[3:4], axis=1, keepdims=True))
                + lam[4:5, 0:1])
    lam_init = lam[4:5, 0:1]
    for h in range(DA_HEAD_GROUP):
        o = acc_ref[2 * h] / l_ref[2 * h] - lam_full * (acc_ref[2 * h + 1] / l_ref[2 * h + 1])
        ms = jnp.mean(o * o, axis=-1, keepdims=True)
        o_ref[:, h * DA_V_DIM:(h + 1) * DA_V_DIM] = (
            o * lax.rsqrt(ms + NORM_EPS) * sub_ref[...] * (1.0 - lam_init))


def _attention(q, k, v, lam_pad, subln, tq):
    bsz, seq, _ = q.shape
    tw = min(2 * tq, seq)
    gw = DA_HEAD_GROUP * DA_V_DIM
    n_maps = 2 * DA_HEAD_GROUP
    kv_spec = pl.BlockSpec((None, seq, gw), lambda b, h, i: (b, 0, h))
    return pl.pallas_call(
        functools.partial(_attn_body, tq=tq, tw=tw),
        grid=(bsz, DA_HEADS // DA_HEAD_GROUP, seq // tq),
        in_specs=[
            pl.BlockSpec((None, tq, gw), lambda b, h, i: (b, i, h)),
            kv_spec, kv_spec,
            _full((SUBLANES, LANES)),
            _full((1, DA_V_DIM)),
        ],
        out_specs=pl.BlockSpec((None, tq, gw), lambda b, h, i: (b, i, h)),
        out_shape=jax.ShapeDtypeStruct((bsz, seq, DA_WIDTH), F32),
        scratch_shapes=[
            pltpu.VMEM((n_maps, tq, LANES), F32),
            pltpu.VMEM((n_maps, tq, LANES), F32),
            pltpu.VMEM((n_maps, tq, DA_V_DIM), F32),
            pltpu.VMEM((n_maps, tq, tw), F32),
            pltpu.VMEM((n_maps, tq, tw), BF16),
        ],
        compiler_params=_cparams(3),
        name="diff_attention",
    )(q, k, v, lam_pad, subln)


def _s5_body(u_ref, bhi_ref, blo_ref, are_ref, aim_ref, chi_ref, clo_ref, d_ref,
             whi_ref, wlo_ref, bg_ref, g_ref, o_ref, st_ref, x_ref, *, tt, bsz):
    ns = SSM_GROUPS * SSM_STATE

    @pl.when(pl.program_id(0) == 0)
    def _():
        st_ref[...] = jnp.zeros(st_ref.shape, F32)

    u = u_ref[...]
    x_ref[...] = _dot_x3(u, bhi_ref[...], blo_ref[...])
    a_re = jnp.broadcast_to(are_ref[...], (bsz, ns))
    a_im = jnp.broadcast_to(aim_ref[...], (bsz, ns))

    def step(t, carry):
        xr, xi = carry
        r = pl.multiple_of(t * bsz, bsz)
        nr = a_re * xr - a_im * xi + x_ref[pl.ds(r, bsz), 0:ns]
        ni = a_re * xi + a_im * xr + x_ref[pl.ds(r, bsz), ns:2 * ns]
        x_ref[pl.ds(r, bsz), 0:ns] = nr
        x_ref[pl.ds(r, bsz), ns:2 * ns] = ni
        return nr, ni

    xr, xi = lax.fori_loop(0, tt, step, (st_ref[:, 0:ns], st_ref[:, ns:2 * ns]))
    st_ref[:, 0:ns] = xr
    st_ref[:, ns:2 * ns] = xi

    y = _dot_x3(x_ref[...], chi_ref[...], clo_ref[...]) + d_ref[...] * u
    yg = _gelu(y)
    z = _dot_x3(yg, whi_ref[...], wlo_ref[...]) + bg_ref[...]
    out = yg * _sigmoid(z)
    ms = jnp.mean(out * out, axis=-1, keepdims=True)
    o_ref[...] = out * lax.rsqrt(ms + NORM_EPS) * g_ref[...]


def _s5(u, ops, tt, bsz):
    rows = u.shape[0]
    ns2 = 2 * SSM_GROUPS * SSM_STATE
    blk = tt * bsz
    io_spec = pl.BlockSpec((blk, SSM_WIDTH), lambda i: (i, 0))
    return pl.pallas_call(
        functools.partial(_s5_body, tt=tt, bsz=bsz),
        grid=(rows // blk,),
        in_specs=[
            io_spec,
            _full((SSM_WIDTH, ns2)), _full((SSM_WIDTH, ns2)),
            _full((1, ns2 // 2)), _full((1, ns2 // 2)),
            _full((ns2, SSM_WIDTH)), _full((ns2, SSM_WIDTH)),
            _full((1, SSM_WIDTH)),
            _full((SSM_WIDTH, SSM_WIDTH)), _full((SSM_WIDTH, SSM_WIDTH)),
            _full((1, SSM_WIDTH)), _full((1, SSM_WIDTH)),
        ],
        out_specs=io_spec,
        out_shape=jax.ShapeDtypeStruct((rows, SSM_WIDTH), F32),
        scratch_shapes=[
            pltpu.VMEM((bsz, ns2), F32),
            pltpu.VMEM((blk, ns2), F32),
        ],
        compiler_params=_cparams(1),
        name="s5",
    )(u, *ops)


def _s5_operands(lam_re, lam_im, b_re, b_im, c_re, c_im, d_skip, log_dt, w_glu, b_glu, out_norm):
    lr = jnp.minimum(lam_re, -1e-4)
    li = lam_im
    step = jnp.exp(log_dt)[:, None]
    mag = jnp.exp(lr * step)
    abar_re = mag * jnp.cos(li * step)
    abar_im = mag * jnp.sin(li * step)
    den = lr * lr + li * li
    coef_re = ((abar_re - 1.0) * lr + abar_im * li) / den
    coef_im = (abar_im * lr - (abar_re - 1.0) * li) / den
    bbar_re = coef_re[..., None] * b_re - coef_im[..., None] * b_im
    bbar_im = coef_re[..., None] * b_im + coef_im[..., None] * b_re
    eye = jnp.eye(SSM_GROUPS, dtype=F32)
    ns = SSM_GROUPS * SSM_STATE

    def in_mat(bb):
        return jnp.einsum('gph,gk->ghkp', bb, eye).reshape(SSM_WIDTH, ns)

    def out_mat(cc):
        return jnp.einsum('ghp,gk->kpgh', cc, eye).reshape(ns, SSM_WIDTH)

    bmat = jnp.concatenate([in_mat(bbar_re), in_mat(bbar_im)], axis=1)
    cmat = jnp.concatenate([out_mat(c_re), -out_mat(c_im)], axis=0)
    return (*_wsplit(bmat), abar_re.reshape(1, ns), abar_im.reshape(1, ns), *_wsplit(cmat),
            d_skip.reshape(1, SSM_WIDTH), *_wsplit(w_glu), b_glu.reshape(1, SSM_WIDTH),
            out_norm.reshape(1, SSM_WIDTH))


def _rwkv_body(c_ref, mu_ref, wdh_ref, wdl_ref, wah_ref, wal_ref, wgh_ref, wgl_ref,
               w0_ref, a0_ref, kk_ref, ka_ref, rk_ref, lnw_ref, lnb_ref,
               seg256_ref, seg128_ref, eye_ref,
               o_ref, prev_ref, st_ref, r_s, w_s, k_s, v_s, a_s, b_s, y_s, *, tt, bsz):
    n_pairs = RW_WIDTH // LANES
    n_inst = n_pairs * bsz
    hd = RW_HEAD_DIM

    @pl.when(pl.program_id(0) == 0)
    def _():
        prev_ref[...] = jnp.zeros(prev_ref.shape, F32)
        st_ref[...] = jnp.zeros(st_ref.shape, F32)

    c = c_ref[...]
    rows = c.shape[0]
    prev = jnp.concatenate([prev_ref[...], c[0:rows - bsz]], axis=0)
    prev_ref[...] = c[rows - bsz:rows]
    cs = c + (prev - c) * mu_ref[...]
    r = cs[:, 0:RW_WIDTH]
    k = cs[:, RW_WIDTH:2 * RW_WIDTH]
    v = cs[:, 2 * RW_WIDTH:3 * RW_WIDTH]
    x3 = cs[:, 3 * RW_WIDTH:RW_COLS]

    seg256 = seg256_ref[...]
    wdec = w0_ref[...] + _dot_x3(jnp.tanh(x3), wdh_ref[...], wdl_ref[...])
    z = -wdec
    softplus = jnp.maximum(z, 0.0) + jnp.log(1.0 + jnp.exp(-jnp.abs(z)))
    decay = jnp.exp(-jnp.exp(-softplus - 0.5))
    a = _sigmoid(a0_ref[...] + _dot_x3(x3, wah_ref[...], wal_ref[...]))
    g = _dot_x3(_sigmoid(x3), wgh_ref[...], wgl_ref[...])
    kk = k * kk_ref[...]
    kk = kk / jnp.maximum(jnp.sqrt(_dot_x2(kk * kk, seg256)), 1e-12)
    k2 = k * (1.0 + (a - 1.0) * ka_ref[...])
    r_s[...] = r
    w_s[...] = decay
    k_s[...] = k2
    v_s[...] = v
    a_s[...] = -kk
    b_s[...] = kk * a

    seg128x2 = seg128_ref[...]
    seg128 = seg128x2[0:LANES]
    eye = eye_ref[...]
    sub = lax.broadcasted_iota(jnp.int32, (bsz, LANES), 0)

    def vec_of(block, inst):
        p, b = divmod(inst, bsz)
        return block[b:b + 1, p * LANES:(p + 1) * LANES]

    def read_out(states, r_blk, dst_row):
        lhs_y = jnp.concatenate([states[i] * vec_of(r_blk, i) for i in range(n_inst)], axis=0)
        ycol = _dot(lhs_y.astype(BF16), seg128)
        for p in range(n_pairs):
            yblk = jnp.zeros((bsz, LANES), F32)
            for b in range(bsz):
                i = p * bsz + b
                yrow = jnp.sum(eye * ycol[i * hd:(i + 1) * hd], axis=0, keepdims=True)
                yblk = jnp.where(sub == b, yrow, yblk)
            y_s[pl.ds(dst_row, bsz), p * LANES:(p + 1) * LANES] = yblk

    def step(t, carry):
        row0 = pl.multiple_of(t * bsz, bsz)
        blk = [ref[pl.ds(row0, bsz), :] for ref in (w_s, k_s, v_s, a_s, b_s)]
        vec = lambda q, inst: vec_of(blk[q], inst)
        states = [st_ref[i * hd:(i + 1) * hd, :] for i in range(n_inst)]
        prev_row = pl.multiple_of(jnp.maximum(row0 - bsz, 0), bsz)
        read_out(states, r_s[pl.ds(prev_row, bsz), :], row0)
        halves = [range(g, g + n_inst // 2) for g in (0, n_inst // 2)]
        lhs_a = [jnp.concatenate([states[i] * vec(3, i) for i in h], axis=0) for h in halves]
        lhs_v = [jnp.concatenate([eye * vec(2, i) for i in h], axis=0) for h in halves]
        sa = [_dot(x.astype(BF16), seg128) for x in lhs_a]
        vcol = [_dot(x.astype(BF16), seg128) for x in lhs_v]
        for g, h in enumerate(halves):
            for n, i in enumerate(h):
                sl = slice(n * hd, (n + 1) * hd)
                st_ref[i * hd:(i + 1) * hd, :] = (states[i] * vec(0, i) + sa[g][sl] * vec(4, i)
                                                  + vcol[g][sl] * vec(1, i))
        return carry

    lax.fori_loop(0, tt, step, 0)
    last = (tt - 1) * bsz
    read_out([st_ref[i * hd:(i + 1) * hd, :] for i in range(n_inst)], r_s[last:last + bsz, :], tt * bsz)

    y = y_s[bsz:(tt + 1) * bsz, :]
    inv = 1.0 / hd
    mean = _dot_x2(y, seg256) * inv
    dv = y - mean
    var = _dot_x2(dv * dv, seg256) * inv
    yn = dv * lax.rsqrt(var + RW_LN_EPS) * lnw_ref[...] + lnb_ref[...]
    bonus = _dot_x2(r * k2 * rk_ref[...], seg256) * v
    o_ref[...] = (yn + bonus) * g


def _rwkv(c, ops, tt, bsz):
    rows = c.shape[0]
    blk = tt * bsz
    vec_spec = _full((1, RW_WIDTH))
    pad_spec = _full((LANES, RW_WIDTH))
    n_inst = (RW_WIDTH // LANES) * bsz
    return pl.pallas_call(
        functools.partial(_rwkv_body, tt=tt, bsz=bsz),
        grid=(rows // blk,),
        in_specs=[
            pl.BlockSpec((blk, RW_COLS), lambda i: (i, 0)),
            _full((1, RW_COLS)),
            pad_spec, pad_spec, pad_spec, pad_spec, pad_spec, pad_spec,
            vec_spec, vec_spec, vec_spec, vec_spec, vec_spec, vec_spec, vec_spec,
            _full((RW_WIDTH, RW_WIDTH)), _full((2 * LANES, LANES)), _full((RW_HEAD_DIM, LANES)),
        ],
        out_specs=pl.BlockSpec((blk, RW_WIDTH), lambda i: (i, 0)),
        out_shape=jax.ShapeDtypeStruct((rows, RW_WIDTH), F32),
        scratch_shapes=[
            pltpu.VMEM((bsz, RW_COLS), F32),
            pltpu.VMEM((n_inst * RW_HEAD_DIM, LANES), F32),
        ] + [pltpu.VMEM((blk, RW_WIDTH), F32)] * 6 + [pltpu.VMEM((blk + bsz, RW_WIDTH), F32)],
        compiler_params=_cparams(1),
        name="rwkv7",
    )(c, *ops)


def _rwkv_operands(mu, decay_up, decay_w0, aaa_up, aaa_a0, gate_up, k_k, k_a, r_k, ln_w, ln_b):
    def pad(w, off):
        return _wsplit(jnp.zeros((LANES, RW_WIDTH), F32).at[off:off + w.shape[0]].set(w))

    row = lambda t: t.reshape(1, RW_WIDTH)
    lane = jnp.arange(LANES)
    eye = (lane[None, :] % RW_HEAD_DIM == jnp.arange(RW_HEAD_DIM)[:, None]).astype(F32)
    return (mu.reshape(1, RW_COLS),
            *pad(decay_up, 0), *pad(aaa_up, RW_DECAY_RANK),
            *pad(gate_up, RW_DECAY_RANK + RW_AAA_RANK),
            row(decay_w0), row(aaa_a0), row(k_k), row(k_a), row(r_k), row(ln_w), row(ln_b),
            _seg_ones(RW_WIDTH, RW_HEAD_DIM),
            jnp.concatenate([_seg_ones(LANES, RW_HEAD_DIM)] * 2, axis=0), eye)


def _outproj_body(x_ref, ya_ref, yb_ref, yc_ref, wa_ref, wb_ref, wc_ref, g_ref, wq_ref,
                  keys_ref, xn_ref, h_ref, st_ref):
    xn = (x_ref[...]
          + _dot(ya_ref[...].astype(BF16), wa_ref[...])
          + _dot(yb_ref[...].astype(BF16), wb_ref[...])
          + _dot(yc_ref[...].astype(BF16), wc_ref[...]))
    xn_ref[...] = xn
    ms = jnp.mean(xn * xn, axis=-1, keepdims=True)
    h = xn * lax.rsqrt(ms + NORM_EPS) * g_ref[...]
    h_ref[...] = h
    q = _dot(h.astype(BF16), wq_ref[...]).astype(BF16)
    for hm in range(2 * PEER_HEADS):
        st_ref[hm * N_KEYS:(hm + 1) * N_KEYS, :] = lax.dot_general(
            keys_ref[hm], q[:, hm * PEER_KEY_DIM:(hm + 1) * PEER_KEY_DIM],
            (((1,), (1,)), ((), ())), preferred_element_type=F32)


def _outproj(x, ya, yb, yc, wa, wb, wc, g, wq, keys, tl):
    bsz, seq, _ = x.shape
    nl = seq // tl
    n_rows = 2 * PEER_HEADS * N_KEYS
    tok_spec = pl.BlockSpec((None, tl, D_MODEL), lambda b, i: (b, i, 0))
    tm_spec = pl.BlockSpec((tl, SSM_WIDTH), lambda b, i: (i, b))
    return pl.pallas_call(
        _outproj_body,
        grid=(bsz, nl),
        in_specs=[
            tok_spec,
            pl.BlockSpec((None, tl, DA_WIDTH), lambda b, i: (b, i, 0)),
            tm_spec, tm_spec,
            _full((DA_WIDTH, D_MODEL)), _full((SSM_WIDTH, D_MODEL)), _full((RW_WIDTH, D_MODEL)),
            _full((1, D_MODEL)),
            _full((D_MODEL, n_rows)),
            _full((2 * PEER_HEADS, N_KEYS, PEER_KEY_DIM)),
        ],
        out_specs=[
            tok_spec, tok_spec,
            pl.BlockSpec((n_rows, tl), lambda b, i: (0, b * nl + i)),
        ],
        out_shape=[
            jax.ShapeDtypeStruct((bsz, seq, D_MODEL), F32),
            jax.ShapeDtypeStruct((bsz, seq, D_MODEL), F32),
            jax.ShapeDtypeStruct((n_rows, bsz * seq), F32),
        ],
        compiler_params=_cparams(2),
        name="outproj_peer_scores",
    )(x, ya, yb, yc, wa, wb, wc, g, wq, keys)


def _topk_rows(s, k, payload=None):
    n, width = s.shape
    rows = lax.broadcasted_iota(jnp.int32, s.shape, 0).astype(F32)
    out_rows = lax.broadcasted_iota(jnp.int32, (k, width), 0)
    vals = jnp.zeros((k, width), F32)
    picks = jnp.zeros((k, width), F32)
    for j in range(k):
        m = jnp.max(s, axis=0, keepdims=True)
        key = jnp.where(s == m, rows, float(n))
        pos = jnp.min(key, axis=0, keepdims=True)
        hit = key == pos
        pick = pos if payload is None else jnp.max(jnp.where(hit, payload, -1.0), axis=0, keepdims=True)
        vals = jnp.where(out_rows == j, m, vals)
        picks = jnp.where(out_rows == j, pick, picks)
        s = jnp.where(hit, -jnp.inf, s)
    return vals, picks


def _candidates(t1, t2, combine, fill):
    k = PEER_TOPK
    sub = lax.broadcasted_iota(jnp.int32, (SUBLANES, t1.shape[1]), 0)
    blocks = [combine(t1[0:1], t2)]
    for a in range(1, SUBLANES):
        n = k // (a + 1)
        blk = combine(t1[a:a + 1], t2[0:SUBLANES])
        blocks.append(blk if n >= SUBLANES else jnp.where(sub < n, blk, fill))
    blocks.append(combine(t1[SUBLANES:k], t2[0:1]))
    return jnp.concatenate(blocks, axis=0)


def _topk_body(st_ref, idx_ref, gate_ref):
    idx_rows, gate_rows = [], []
    for h in range(PEER_HEADS):
        v1, i1 = _topk_rows(st_ref[(2 * h) * N_KEYS:(2 * h + 1) * N_KEYS, :], PEER_TOPK)
        v2, i2 = _topk_rows(st_ref[(2 * h + 1) * N_KEYS:(2 * h + 2) * N_KEYS, :], PEER_TOPK)
        cand_s = _candidates(v1, v2, lambda x, y: x + y, -jnp.inf)
        cand_e = _candidates(i1, i2, lambda x, y: x * N_KEYS + y, -1.0)
        best, expert = _topk_rows(cand_s, PEER_TOPK, payload=cand_e)
        e = jnp.exp(best - best[0:1])
        gate_rows.append(e / jnp.sum(e, axis=0, keepdims=True))
        idx_rows.append((expert * PEER_ROW_SUBLANES).astype(jnp.int32))
    gate_ref[...] = jnp.concatenate(gate_rows, axis=0)
    idx_ref[...] = jnp.concatenate(idx_rows, axis=0).T


def _topk(scores_t, tb):
    t = scores_t.shape[1]
    return pl.pallas_call(
        _topk_body,
        grid=(t // tb,),
        in_specs=[pl.BlockSpec((scores_t.shape[0], tb), lambda i: (0, i))],
        out_specs=[
            pl.BlockSpec((tb, PEER_SLOTS), lambda i: (i, 0)),
            pl.BlockSpec((PEER_SLOTS, tb), lambda i: (0, i)),
        ],
        out_shape=[
            jax.ShapeDtypeStruct((t, PEER_SLOTS), jnp.int32),
            jax.ShapeDtypeStruct((PEER_SLOTS, t), F32),
        ],
        compiler_params=_cparams(1),
        name="peer_topk",
    )(scores_t)


def _pack_body(t_ref, o_ref):
    n = t_ref.shape[0]
    bits = lax.bitcast_convert_type(t_ref[...].astype(BF16).astype(F32), jnp.uint32)
    words = (bits[:, :PEER_ROW_WORDS] >> 16) | (bits[:, PEER_ROW_WORDS:] & jnp.uint32(0xFFFF0000))
    for r in range(PEER_ROW_SUBLANES):
        o_ref[pl.ds(r, n, stride=PEER_ROW_SUBLANES), :] = words[:, r * LANES:(r + 1) * LANES]


def _pack_table(tab, te=512):
    n = tab.shape[0]
    return pl.pallas_call(
        _pack_body,
        grid=(n // te,),
        in_specs=[pl.BlockSpec((te, D_MODEL), lambda i: (i, 0))],
        out_specs=pl.BlockSpec((te * PEER_ROW_SUBLANES, LANES), lambda i: (i, 0)),
        out_shape=jax.ShapeDtypeStruct((n * PEER_ROW_SUBLANES, LANES), jnp.uint32),
        compiler_params=_cparams(1),
        name="peer_pack_table",
    )(tab)


PEER_TILE_ROWS = PEER_SLOTS * PEER_ROW_SUBLANES
PEER_UNROLL = SUBLANES


def _gather_rows(tab_ref, idx_ref, tt, dst_ref, u):
    rs = PEER_ROW_SUBLANES
    for k in range(PEER_SLOTS):
        off = pl.multiple_of(idx_ref[tt, k], rs)
        dst_ref[pl.ds(u * PEER_TILE_ROWS + k * rs, rs), :] = tab_ref[pl.ds(off, rs), :]


def _row_plane(rows_ref, u, r):
    words = rows_ref[pl.ds(u * PEER_TILE_ROWS + r, PEER_SLOTS, stride=PEER_ROW_SUBLANES), :]
    lo = lax.bitcast_convert_type(words << 16, F32)
    hi = lax.bitcast_convert_type(words & jnp.uint32(0xFFFF0000), F32)
    return lo, hi


def _table_spec(tab):
    return pl.BlockSpec(tab.shape, lambda i: (0, 0), pipeline_mode=pl.Buffered(1))


def _peer_act_body(idx_ref, h_ref, gate_ref, tab_ref, coef_ref, rows_ref, act_ref, *, tb):
    lane = lax.broadcasted_iota(jnp.int32, (PEER_SLOTS, tb), 1)
    half = D_MODEL // 2
    act_ref[...] = jnp.zeros(act_ref.shape, F32)

    def group(g, carry):
        row0 = pl.multiple_of(g * PEER_UNROLL, PEER_UNROLL)
        h8 = h_ref[pl.ds(row0, PEER_UNROLL), :]
        for u in range(PEER_UNROLL):
            _gather_rows(tab_ref, idx_ref, row0 + u, rows_ref, u)
            part = jnp.zeros((PEER_SLOTS, LANES), F32)
            for r in range(PEER_ROW_SUBLANES):
                lo, hi = _row_plane(rows_ref, u, r)
                part = (part + lo * h8[u:u + 1, r * LANES:(r + 1) * LANES]
                        + hi * h8[u:u + 1, half + r * LANES:half + (r + 1) * LANES])
            act = jnp.sum(part, axis=1, keepdims=True)
            act_ref[...] = jnp.where(lane == row0 + u, act, act_ref[...])
        return carry

    lax.fori_loop(0, tb // PEER_UNROLL, group, 0)
    coef_ref[...] = _gelu(act_ref[...]) * gate_ref[...]


def _peer_act(idx, h, gate_t, tab, tb):
    t = idx.shape[0]
    slot_major = pl.BlockSpec((PEER_SLOTS, tb), lambda i: (0, i))
    return pl.pallas_call(
        functools.partial(_peer_act_body, tb=tb),
        grid=(t // tb,),
        in_specs=[
            pl.BlockSpec((tb, PEER_SLOTS), lambda i: (i, 0), memory_space=pltpu.SMEM),
            pl.BlockSpec((tb, D_MODEL), lambda i: (i, 0)),
            slot_major,
            _table_spec(tab),
        ],
        out_specs=slot_major,
        out_shape=jax.ShapeDtypeStruct((PEER_SLOTS, t), F32),
        scratch_shapes=[
            pltpu.VMEM((PEER_UNROLL * PEER_TILE_ROWS, LANES), jnp.uint32),
            pltpu.VMEM((PEER_SLOTS, tb), F32),
        ],
        compiler_params=_cparams(1),
        name="peer_act",
    )(idx, h, gate_t, tab)


def _peer_out_body(idx_ref, coef_ref, x_ref, ones_ref, tab_ref, o_ref, rows_ref, *, tb):
    lane = lax.broadcasted_iota(jnp.int32, (PEER_SLOTS, tb), 1)
    sub = lax.broadcasted_iota(jnp.int32, (PEER_UNROLL, LANES), 0)
    ones = ones_ref[...]
    n_planes = PEER_ROW_SUBLANES

    def group(g, carry):
        row0 = pl.multiple_of(g * PEER_UNROLL, PEER_UNROLL)
        planes = [jnp.zeros((PEER_UNROLL, LANES), F32) for _ in range(2 * n_planes)]
        for u in range(PEER_UNROLL):
            _gather_rows(tab_ref, idx_ref, row0 + u, rows_ref, u)
            ccol = _dot_x2k(jnp.where(lane == row0 + u, coef_ref[...], 0.0), ones)
            for r in range(n_planes):
                lo, hi = _row_plane(rows_ref, u, r)
                planes[r] = jnp.where(sub == u, jnp.sum(lo * ccol, axis=0, keepdims=True), planes[r])
                planes[n_planes + r] = jnp.where(
                    sub == u, jnp.sum(hi * ccol, axis=0, keepdims=True), planes[n_planes + r])
        o_ref[pl.ds(row0, PEER_UNROLL), :] = (x_ref[pl.ds(row0, PEER_UNROLL), :]
                                              + jnp.concatenate(planes, axis=1))
        return carry

    lax.fori_loop(0, tb // PEER_UNROLL, group, 0)


def _peer_out(idx, coef_t, x, tab, tb):
    t = idx.shape[0]
    tok_spec = pl.BlockSpec((tb, D_MODEL), lambda i: (i, 0))
    ones = jnp.ones((2 * tb, LANES), BF16)
    return pl.pallas_call(
        functools.partial(_peer_out_body, tb=tb),
        grid=(t // tb,),
        in_specs=[
            pl.BlockSpec((tb, PEER_SLOTS), lambda i: (i, 0), memory_space=pltpu.SMEM),
            pl.BlockSpec((PEER_SLOTS, tb), lambda i: (0, i)),
            tok_spec,
            _full(ones.shape),
            _table_spec(tab),
        ],
        out_specs=tok_spec,
        out_shape=jax.ShapeDtypeStruct(x.shape, F32),
        scratch_shapes=[pltpu.VMEM((PEER_UNROLL * PEER_TILE_ROWS, LANES), jnp.uint32)],
        compiler_params=_cparams(1),
        name="peer_out",
    )(idx, coef_t, x, ones, tab)


def _tiles(bsz, seq):
    tl = min(seq, 512)
    tq = min(seq, 256)
    tt_s5 = min(seq, 128)
    tt_rw = min(seq, 64)
    tb = 128
    return tl, tq, tt_s5, tt_rw, tb


def kernel(x, positions, norm_mix, w_in, da_q_norm, da_k_norm, da_lambda, da_subln, ssm_lambda_re, ssm_lambda_im, ssm_b_re, ssm_b_im, ssm_c_re, ssm_c_im, ssm_d, ssm_log_dt, ssm_w_glu, ssm_b_glu, ssm_out_norm, rw_mu, rw_decay_up, rw_decay_w0, rw_aaa_up, rw_aaa_a0, rw_gate_up, rw_k_k, rw_k_a, rw_r_k, rw_ln_w, rw_ln_b, w_out, norm_ffn, peer_w_q, peer_sub_keys, peer_u, peer_v):
    bsz, seq, _ = x.shape
    depth = w_in.shape[0]
    t = bsz * seq
    tl, tq, tt_s5, tt_rw, tb = _tiles(bsz, seq)
    rope = _rope_tables(positions, tl)

    for l in range(depth):
        w = w_in[l].astype(BF16)
        cols_a, cols_b, cols_c = _inproj(
            x, norm_mix[l].reshape(1, D_MODEL), w[:, :DA_COLS],
            w[:, DA_COLS:DA_COLS + SSM_WIDTH], w[:, DA_COLS + SSM_WIDTH:], tl)

        q, k, v = _attn_prep(cols_a.reshape(t, DA_COLS), rope,
                             da_q_norm[l].reshape(1, DA_WIDTH), da_k_norm[l].reshape(1, DA_WIDTH), tl)
        lam_init = 0.8 - 0.6 * math.exp(-0.3 * l)
        lam_pad = (jnp.zeros((SUBLANES, LANES), F32)
                   .at[0:4, 0:DA_HEAD_DIM].set(da_lambda[l]).at[4, :].set(lam_init))
        y_a = _attention(q.reshape(bsz, seq, DA_WIDTH), k.reshape(bsz, seq, DA_WIDTH),
                         v.reshape(bsz, seq, DA_WIDTH), lam_pad, da_subln[l].reshape(1, DA_V_DIM), tq)

        s5_ops = _s5_operands(ssm_lambda_re[l], ssm_lambda_im[l], ssm_b_re[l], ssm_b_im[l],
                              ssm_c_re[l], ssm_c_im[l], ssm_d[l], ssm_log_dt[l], ssm_w_glu[l],
                              ssm_b_glu[l], ssm_out_norm[l])
        y_b = _s5(cols_b.reshape(t, SSM_WIDTH), s5_ops, tt_s5, bsz)

        rw_ops = _rwkv_operands(rw_mu[l], rw_decay_up[l], rw_decay_w0[l], rw_aaa_up[l], rw_aaa_a0[l],
                                rw_gate_up[l], rw_k_k[l], rw_k_a[l], rw_r_k[l], rw_ln_w[l], rw_ln_b[l])
        y_c = _rwkv(cols_c.reshape(t, RW_COLS), rw_ops, tt_rw, bsz)

        wo = w_out[l].astype(BF16)
        keys = peer_sub_keys[l].reshape(2 * PEER_HEADS, N_KEYS, PEER_KEY_DIM).astype(BF16)
        x_mid, h_ffn, scores_t = _outproj(
            x, y_a, y_b.reshape(seq, bsz * SSM_WIDTH), y_c.reshape(seq, bsz * RW_WIDTH),
            wo[:DA_WIDTH], wo[DA_WIDTH:DA_WIDTH + SSM_WIDTH], wo[DA_WIDTH + SSM_WIDTH:],
            norm_ffn[l].reshape(1, D_MODEL), peer_w_q[l].astype(BF16), keys, tl)

        idx, gate_t = _topk(scores_t, tb)
        coef_t = _peer_act(idx, h_ffn.reshape(t, D_MODEL), gate_t, _pack_table(peer_u[l]), tb)
        x = _peer_out(idx, coef_t, x_mid.reshape(t, D_MODEL),
                      _pack_table(peer_v[l]), tb).reshape(bsz, seq, D_MODEL)
    return x
```

```python
import functools
import math

import jax
import jax.numpy as jnp
from jax import lax
from jax.experimental import pallas as pl
from jax.experimental.pallas import tpu as pltpu

D_MODEL = 1024
CHUNK = 64
NORM_EPS = 1e-6

DA_HEADS = 4
DA_HEAD_DIM = 64
DA_V_DIM = 128
DA_WIDTH = 512
DA_HEAD_GROUP = 4
ROT_DIM = 16
ROPE_THETA = 500000.0

SSM_GROUP = 16
SSM_GROUPS = 16
SSM_WIDTH = 256
SSM_STATE = 64

RW_HEADS = 4
RW_HEAD_DIM = 64
RW_WIDTH = 256
RW_DECAY_RANK = 32
RW_AAA_RANK = 32
RW_GATE_RANK = 64
RW_LN_EPS = 64e-5
RW_COLS = 896

DA_COLS = 1536

PEER_HEADS = 8
N_KEYS = 128
PEER_KEY_DIM = 128
PEER_TOPK = 16
PEER_SLOTS = PEER_HEADS * PEER_TOPK
PEER_ROW_WORDS = D_MODEL // 2
PEER_ROW_SUBLANES = PEER_ROW_WORDS // 128

LANES = 128
SUBLANES = 8
VMEM_LIMIT_BYTES = 56 * 1024 * 1024

F32 = jnp.float32
BF16 = jnp.bfloat16


def _cparams(n_axes, vmem=VMEM_LIMIT_BYTES):
    return pltpu.CompilerParams(
        dimension_semantics=("arbitrary",) * n_axes, vmem_limit_bytes=vmem)


def _dot(a, b):
    return jnp.dot(a, b, preferred_element_type=F32)


def _split(x):
    hi = x.astype(BF16)
    lo = (x - hi.astype(F32)).astype(BF16)
    return hi, lo


def _dot_x2(x, w):
    hi, lo = _split(x)
    return _dot(hi, w) + _dot(lo, w)


def _dot_x2k(x, w_stacked):
    hi, lo = _split(x)
    return _dot(jnp.concatenate([hi, lo], axis=1), w_stacked)


def _dot_x3(x, w_hi, w_lo):
    hi, lo = _split(x)
    return _dot(hi, w_hi) + _dot(lo, w_hi) + _dot(hi, w_lo)


def _wsplit(w):
    hi = w.astype(BF16)
    lo = (w - hi.astype(F32)).astype(BF16)
    return hi, lo


def _sigmoid(x):
    return 1.0 / (1.0 + jnp.exp(-x))


def _gelu(x):
    return 0.5 * x * (1.0 + lax.erf(x * (1.0 / math.sqrt(2.0))))


def _seg_ones(n, seg):
    r = jnp.arange(n) // seg
    return (r[:, None] == r[None, :]).astype(BF16)


def _full(shape):
    nd = len(shape)
    return pl.BlockSpec(shape, lambda *_: (0,) * nd)


def _inproj_body(x_ref, g_ref, wa_ref, wb_ref, wc_ref, a_ref, b_ref, c_ref):
    x = x_ref[...]
    ms = jnp.mean(x * x, axis=-1, keepdims=True)
    h = (x * lax.rsqrt(ms + NORM_EPS) * g_ref[...]).astype(BF16)
    a_ref[...] = _dot(h, wa_ref[...])
    b_ref[...] = _dot(h, wb_ref[...])
    c_ref[...] = _dot(h, wc_ref[...])


def _inproj(x, g, wa, wb, wc, tl):
    bsz, seq, _ = x.shape
    return pl.pallas_call(
        _inproj_body,
        grid=(bsz, seq // tl),
        in_specs=[
            pl.BlockSpec((None, tl, D_MODEL), lambda b, i: (b, i, 0)),
            _full((1, D_MODEL)),
            _full((D_MODEL, DA_COLS)),
            _full((D_MODEL, SSM_WIDTH)),
            _full((D_MODEL, RW_COLS)),
        ],
        out_specs=[
            pl.BlockSpec((None, tl, DA_COLS), lambda b, i: (b, i, 0)),
            pl.BlockSpec((tl, SSM_WIDTH), lambda b, i: (i, b)),
            pl.BlockSpec((tl, RW_COLS), lambda b, i: (i, b)),
        ],
        out_shape=[
            jax.ShapeDtypeStruct((bsz, seq, DA_COLS), F32),
            jax.ShapeDtypeStruct((seq, bsz * SSM_WIDTH), F32),
            jax.ShapeDtypeStruct((seq, bsz * RW_COLS), F32),
        ],
        compiler_params=_cparams(2),
        name="inproj",
    )(x, g, wa, wb, wc)


def _rope_body(pos_ref, invf_ref, c_ref, s1_ref, s2_ref):
    ang = pos_ref[...].astype(F32) * invf_ref[...]
    c, s = jnp.cos(ang), jnp.sin(ang)
    lm = lax.broadcasted_iota(jnp.int32, ang.shape, 1) % DA_HEAD_DIM
    half = ROT_DIM // 2
    c_ref[...] = jnp.where(lm < ROT_DIM, c, 1.0)
    s1_ref[...] = jnp.where(lm < half, -s, 0.0)
    s2_ref[...] = jnp.where((lm >= half) & (lm < ROT_DIM), s, 0.0)


def _rope_tables(positions, tm):
    t = positions.size
    half = ROT_DIM // 2
    inv_freq = ROPE_THETA ** (-jnp.arange(0, ROT_DIM, 2, dtype=F32) / ROT_DIM)
    invf = inv_freq[jnp.arange(LANES) % half][None, :]
    pos = jnp.broadcast_to(positions.reshape(t, 1), (t, LANES))
    spec = pl.BlockSpec((tm, LANES), lambda i: (i, 0))
    return pl.pallas_call(
        _rope_body,
        grid=(t // tm,),
        in_specs=[spec, _full((1, LANES))],
        out_specs=[spec, spec, spec],
        out_shape=[jax.ShapeDtypeStruct((t, LANES), F32)] * 3,
        compiler_params=_cparams(1),
        name="rope_tables",
    )(pos, invf)


def _attn_prep_body(a_ref, c_ref, s1_ref, s2_ref, qg_ref, kg_ref, seg_ref,
                    q_ref, k_ref, v_ref):
    reps = DA_WIDTH // LANES
    c = jnp.tile(c_ref[...], (1, reps))
    s1 = jnp.tile(s1_ref[...], (1, reps))
    s2 = jnp.tile(s2_ref[...], (1, reps))
    seg = seg_ref[...]
    half = ROT_DIM // 2

    def norm_rope(t, gain):
        ms = _dot_x2(t * t, seg) * (1.0 / DA_HEAD_DIM)
        tn = t * lax.rsqrt(ms + NORM_EPS) * gain
        up = pltpu.roll(tn, DA_WIDTH - half, axis=1)
        dn = pltpu.roll(tn, half, axis=1)
        return tn * c + up * s1 + dn * s2

    q = norm_rope(a_ref[:, 0:DA_WIDTH], qg_ref[...])
    k = norm_rope(a_ref[:, DA_WIDTH:2 * DA_WIDTH], kg_ref[...])
    q_ref[...] = (q * (DA_HEAD_DIM ** -0.5)).astype(BF16)
    k_ref[...] = k.astype(BF16)
    v_ref[...] = a_ref[:, 2 * DA_WIDTH:3 * DA_WIDTH].astype(BF16)


def _attn_prep(cols_a, rope, qg, kg, tm):
    t = cols_a.shape[0]
    c, s1, s2 = rope
    lane_spec = pl.BlockSpec((tm, LANES), lambda i: (i, 0))
    out_spec = pl.BlockSpec((tm, DA_WIDTH), lambda i: (i, 0))
    return pl.pallas_call(
        _attn_prep_body,
        grid=(t // tm,),
        in_specs=[
            pl.BlockSpec((tm, DA_COLS), lambda i: (i, 0)),
            lane_spec, lane_spec, lane_spec,
            _full((1, DA_WIDTH)), _full((1, DA_WIDTH)),
            _full((DA_WIDTH, DA_WIDTH)),
        ],
        out_specs=[out_spec, out_spec, out_spec],
        out_shape=[jax.ShapeDtypeStruct((t, DA_WIDTH), BF16)] * 3,
        compiler_params=_cparams(1),
        name="attn_prep",
    )(cols_a, c, s1, s2, qg, kg, _seg_ones(DA_WIDTH, DA_HEAD_DIM))


def _attn_body(q_ref, k_ref, v_ref, lam_ref, sub_ref, o_ref, m_ref, l_ref, acc_ref, s_ref, p_ref,
               *, tq, tw):
    qi = pl.program_id(2)
    lane = lax.broadcasted_iota(jnp.int32, (tq, DA_V_DIM), 1)
    q2 = []
    for h in range(DA_HEAD_GROUP):
        q = q_ref[:, h * DA_V_DIM:(h + 1) * DA_V_DIM]
        zero = jnp.zeros_like(q)
        q2.append(jnp.concatenate([jnp.where(lane < DA_HEAD_DIM, q, zero),
                                   jnp.where(lane >= DA_HEAD_DIM, q, zero)], axis=0))

    m_ref[...] = jnp.full(m_ref.shape, -jnp.inf, F32)
    l_ref[...] = jnp.zeros(l_ref.shape, F32)
    acc_ref[...] = jnp.zeros(acc_ref.shape, F32)

    def span(c0, masked):
        c0 = pl.multiple_of(c0, tw)
        if masked:
            row_chunk = (lax.broadcasted_iota(jnp.int32, (tq, tw), 0) + tq) // CHUNK
            col_chunk = (lax.broadcasted_iota(jnp.int32, (tq, tw), 1) + (c0 - qi * tq + tq)) // CHUNK
            visible = col_chunk <= row_chunk
        for h in range(DA_HEAD_GROUP):
            k = k_ref[pl.ds(c0, tw), h * DA_V_DIM:(h + 1) * DA_V_DIM]
            s2 = lax.dot_general(q2[h], k, (((1,), (1,)), ((), ())), preferred_element_type=F32)
            for m in range(2):
                s = s2[m * tq:(m + 1) * tq]
                s_ref[2 * h + m] = jnp.where(visible, s, -jnp.inf) if masked else s
        for hm in range(2 * DA_HEAD_GROUP):
            s = s_ref[hm]
            m_old = m_ref[hm]
            m_new = jnp.maximum(m_old, jnp.max(s, axis=1, keepdims=True))
            alpha = jnp.exp(m_old - m_new)
            p = jnp.exp(s - jnp.tile(m_new, (1, tw // LANES)))
            p_ref[hm] = p.astype(BF16)
            l_ref[hm] = alpha * l_ref[hm] + jnp.sum(p, axis=1, keepdims=True)
            acc_ref[hm] = alpha * acc_ref[hm]
            m_ref[hm] = m_new
        for h in range(DA_HEAD_GROUP):
            v = v_ref[pl.ds(c0, tw), h * DA_V_DIM:(h + 1) * DA_V_DIM]
            pv = _dot(jnp.concatenate([p_ref[2 * h], p_ref[2 * h + 1]], axis=0), v)
            for m in range(2):
                acc_ref[2 * h + m] += pv[m * tq:(m + 1) * tq]

    def full_span(c, carry):
        span(c * tw, False)
        return carry

    n_full = qi // (tw // tq)
    lax.fori_loop(0, n_full, full_span, 0)
    span(n_full * tw, True)

    lam = lam_ref[...]
    lam_full = (jnp.exp(jnp.sum(lam[0:1] * lam[1:2], axis=1, keepdims=True))
                - jnp.exp(jnp.sum(lam[2:3] * lam[3:4], axis=1, keepdims=True))
                + lam[4:5, 0:1])
    lam_init = lam[4:5, 0:1]
    for h in range(DA_HEAD_GROUP):
        o = acc_ref[2 * h] / l_ref[2 * h] - lam_full * (acc_ref[2 * h + 1] / l_ref[2 * h + 1])
        ms = jnp.mean(o * o, axis=-1, keepdims=True)
        o_ref[:, h * DA_V_DIM:(h + 1) * DA_V_DIM] = (
            o * lax.rsqrt(ms + NORM_EPS) * sub_ref[...] * (1.0 - lam_init))


def _attention(q, k, v, lam_pad, subln, tq):
    bsz, seq, _ = q.shape
    tw = min(2 * tq, seq)
    gw = DA_HEAD_GROUP * DA_V_DIM
    n_maps = 2 * DA_HEAD_GROUP
    kv_spec = pl.BlockSpec((None, seq, gw), lambda b, h, i: (b, 0, h))
    return pl.pallas_call(
        functools.partial(_attn_body, tq=tq, tw=tw),
        grid=(bsz, DA_HEADS // DA_HEAD_GROUP, seq // tq),
        in_specs=[
            pl.BlockSpec((None, tq, gw), lambda b, h, i: (b, i, h)),
            kv_spec, kv_spec,
            _full((SUBLANES, LANES)),
            _full((1, DA_V_DIM)),
        ],
        out_specs=pl.BlockSpec((None, tq, gw), lambda b, h, i: (b, i, h)),
        out_shape=jax.ShapeDtypeStruct((bsz, seq, DA_WIDTH), F32),
        scratch_shapes=[
            pltpu.VMEM((n_maps, tq, LANES), F32),
            pltpu.VMEM((n_maps, tq, LANES), F32),
            pltpu.VMEM((n_maps, tq, DA_V_DIM), F32),
            pltpu.VMEM((n_maps, tq, tw), F32),
            pltpu.VMEM((n_maps, tq, tw), BF16),
        ],
        compiler_params=_cparams(3),
        name="diff_attention",
    )(q, k, v, lam_pad, subln)


def _s5_body(u_ref, b_ref, are_ref, aim_ref, c_ref, d_ref, w_ref, bg_ref, g_ref, o_ref, st_ref, x_ref,
             *, tt, bsz):
    ns = SSM_GROUPS * SSM_STATE

    @pl.when(pl.program_id(0) == 0)
    def _():
        st_ref[...] = jnp.zeros(st_ref.shape, F32)

    u = u_ref[...]
    x_ref[...] = _dot(u.astype(BF16), b_ref[...])
    a_re = jnp.broadcast_to(are_ref[...], (bsz, ns))
    a_im = jnp.broadcast_to(aim_ref[...], (bsz, ns))

    def step(t, carry):
        xr, xi = carry
        r = pl.multiple_of(t * bsz, bsz)
        nr = a_re * xr - a_im * xi + x_ref[pl.ds(r, bsz), 0:ns]
        ni = a_re * xi + a_im * xr + x_ref[pl.ds(r, bsz), ns:2 * ns]
        x_ref[pl.ds(r, bsz), 0:ns] = nr
        x_ref[pl.ds(r, bsz), ns:2 * ns] = ni
        return nr, ni

    xr, xi = lax.fori_loop(0, tt, step, (st_ref[:, 0:ns], st_ref[:, ns:2 * ns]))
    st_ref[:, 0:ns] = xr
    st_ref[:, ns:2 * ns] = xi

    y = _dot(x_ref[...].astype(BF16), c_ref[...]) + d_ref[...] * u
    yg = _gelu(y)
    z = _dot(yg.astype(BF16), w_ref[...]) + bg_ref[...]
    out = yg * _sigmoid(z)
    ms = jnp.mean(out * out, axis=-1, keepdims=True)
    o_ref[...] = out * lax.rsqrt(ms + NORM_EPS) * g_ref[...]


def _s5(u, ops, tt, bsz):
    rows = u.shape[0]
    ns2 = 2 * SSM_GROUPS * SSM_STATE
    blk = tt * bsz
    io_spec = pl.BlockSpec((blk, SSM_WIDTH), lambda i: (i, 0))
    return pl.pallas_call(
        functools.partial(_s5_body, tt=tt, bsz=bsz),
        grid=(rows // blk,),
        in_specs=[
            io_spec,
            _full((SSM_WIDTH, ns2)),
            _full((1, ns2 // 2)), _full((1, ns2 // 2)),
            _full((ns2, SSM_WIDTH)),
            _full((1, SSM_WIDTH)),
            _full((SSM_WIDTH, SSM_WIDTH)),
            _full((1, SSM_WIDTH)), _full((1, SSM_WIDTH)),
        ],
        out_specs=io_spec,
        out_shape=jax.ShapeDtypeStruct((rows, SSM_WIDTH), F32),
        scratch_shapes=[
            pltpu.VMEM((bsz, ns2), F32),
            pltpu.VMEM((blk, ns2), F32),
        ],
        compiler_params=_cparams(1),
        name="s5",
    )(u, *ops)


def _s5_operands(lam_re, lam_im, b_re, b_im, c_re, c_im, d_skip, log_dt, w_glu, b_glu, out_norm):
    lr = jnp.minimum(lam_re, -1e-4)
    li = lam_im
    step = jnp.exp(log_dt)[:, None]
    mag = jnp.exp(lr * step)
    abar_re = mag * jnp.cos(li * step)
    abar_im = mag * jnp.sin(li * step)
    den = lr * lr + li * li
    coef_re = ((abar_re - 1.0) * lr + abar_im * li) / den
    coef_im = (abar_im * lr - (abar_re - 1.0) * li) / den
    bbar_re = coef_re[..., None] * b_re - coef_im[..., None] * b_im
    bbar_im = coef_re[..., None] * b_im + coef_im[..., None] * b_re
    eye = jnp.eye(SSM_GROUPS, dtype=F32)
    ns = SSM_GROUPS * SSM_STATE

    def in_mat(bb):
        return jnp.einsum('gph,gk->ghkp', bb, eye).reshape(SSM_WIDTH, ns)

    def out_mat(cc):
        return jnp.einsum('ghp,gk->kpgh', cc, eye).reshape(ns, SSM_WIDTH)

    bmat = jnp.concatenate([in_mat(bbar_re), in_mat(bbar_im)], axis=1)
    cmat = jnp.concatenate([out_mat(c_re), -out_mat(c_im)], axis=0)
    return (bmat.astype(BF16), abar_re.reshape(1, ns), abar_im.reshape(1, ns), cmat.astype(BF16),
            d_skip.reshape(1, SSM_WIDTH), w_glu.astype(BF16), b_glu.reshape(1, SSM_WIDTH),
            out_norm.reshape(1, SSM_WIDTH))


def _rwkv_body(c_ref, mu_ref, wdh_ref, wdl_ref, wah_ref, wal_ref, wgh_ref, wgl_ref,
               w0_ref, a0_ref, kk_ref, ka_ref, rk_ref, lnw_ref, lnb_ref,
               seg256_ref, seg128_ref, eye_ref,
               o_ref, prev_ref, st_ref, r_s, w_s, k_s, v_s, a_s, b_s, y_s, *, tt, bsz):
    n_pairs = RW_WIDTH // LANES
    n_inst = n_pairs * bsz
    hd = RW_HEAD_DIM

    @pl.when(pl.program_id(0) == 0)
    def _():
        prev_ref[...] = jnp.zeros(prev_ref.shape, F32)
        st_ref[...] = jnp.zeros(st_ref.shape, F32)

    c = c_ref[...]
    rows = c.shape[0]
    prev = jnp.concatenate([prev_ref[...], c[0:rows - bsz]], axis=0)
    prev_ref[...] = c[rows - bsz:rows]
    cs = c + (prev - c) * mu_ref[...]
    r = cs[:, 0:RW_WIDTH]
    k = cs[:, RW_WIDTH:2 * RW_WIDTH]
    v = cs[:, 2 * RW_WIDTH:3 * RW_WIDTH]
    x3 = cs[:, 3 * RW_WIDTH:RW_COLS]

    seg256 = seg256_ref[...]
    wdec = w0_ref[...] + _dot_x3(jnp.tanh(x3), wdh_ref[...], wdl_ref[...])
    z = -wdec
    softplus = jnp.maximum(z, 0.0) + jnp.log(1.0 + jnp.exp(-jnp.abs(z)))
    decay = jnp.exp(-jnp.exp(-softplus - 0.5))
    a = _sigmoid(a0_ref[...] + _dot_x3(x3, wah_ref[...], wal_ref[...]))
    g = _dot_x3(_sigmoid(x3), wgh_ref[...], wgl_ref[...])
    kk = k * kk_ref[...]
    kk = kk / jnp.maximum(jnp.sqrt(_dot_x2(kk * kk, seg256)), 1e-12)
    k2 = k * (1.0 + (a - 1.0) * ka_ref[...])
    r_s[...] = r
    w_s[...] = decay
    k_s[...] = k2
    v_s[...] = v
    a_s[...] = -kk
    b_s[...] = kk * a

    seg128x2 = seg128_ref[...]
    seg128 = seg128x2[0:LANES]
    eye = eye_ref[...]
    sub = lax.broadcasted_iota(jnp.int32, (bsz, LANES), 0)

    def vec_of(block, inst):
        p, b = divmod(inst, bsz)
        return block[b:b + 1, p * LANES:(p + 1) * LANES]

    def read_out(states, r_blk, dst_row):
        lhs_y = jnp.concatenate([states[i] * vec_of(r_blk, i) for i in range(n_inst)], axis=0)
        ycol = _dot(lhs_y.astype(BF16), seg128)
        for p in range(n_pairs):
            yblk = jnp.zeros((bsz, LANES), F32)
            for b in range(bsz):
                i = p * bsz + b
                yrow = jnp.sum(eye * ycol[i * hd:(i + 1) * hd], axis=0, keepdims=True)
                yblk = jnp.where(sub == b, yrow, yblk)
            y_s[pl.ds(dst_row, bsz), p * LANES:(p + 1) * LANES] = yblk

    def step(t, carry):
        row0 = pl.multiple_of(t * bsz, bsz)
        blk = [ref[pl.ds(row0, bsz), :] for ref in (w_s, k_s, v_s, a_s, b_s)]
        vec = lambda q, inst: vec_of(blk[q], inst)
        states = [st_ref[i * hd:(i + 1) * hd, :] for i in range(n_inst)]
        prev_row = pl.multiple_of(jnp.maximum(row0 - bsz, 0), bsz)
        read_out(states, r_s[pl.ds(prev_row, bsz), :], row0)
        halves = [range(g, g + n_inst // 2) for g in (0, n_inst // 2)]
        lhs_a = [jnp.concatenate([states[i] * vec(3, i) for i in h], axis=0) for h in halves]
        lhs_v = [jnp.concatenate([eye * vec(2, i) for i in h], axis=0) for h in halves]
        sa = [_dot(x.astype(BF16), seg128) for x in lhs_a]
        vcol = [_dot(x.astype(BF16), seg128) for x in lhs_v]
        for g, h in enumerate(halves):
            for n, i in enumerate(h):
                sl = slice(n * hd, (n + 1) * hd)
                st_ref[i * hd:(i + 1) * hd, :] = (states[i] * vec(0, i) + sa[g][sl] * vec(4, i)
                                                  + vcol[g][sl] * vec(1, i))
        return carry

    lax.fori_loop(0, tt, step, 0)
    last = (tt - 1) * bsz
    read_out([st_ref[i * hd:(i + 1) * hd, :] for i in range(n_inst)], r_s[last:last + bsz, :], tt * bsz)

    y = y_s[bsz:(tt + 1) * bsz, :]
    inv = 1.0 / hd
    mean = _dot_x2(y, seg256) * inv
    dv = y - mean
    var = _dot_x2(dv * dv, seg256) * inv
    yn = dv * lax.rsqrt(var + RW_LN_EPS) * lnw_ref[...] + lnb_ref[...]
    bonus = _dot_x2(r * k2 * rk_ref[...], seg256) * v
    o_ref[...] = (yn + bonus) * g


def _rwkv(c, ops, tt, bsz):
    rows = c.shape[0]
    blk = tt * bsz
    vec_spec = _full((1, RW_WIDTH))
    pad_spec = _full((LANES, RW_WIDTH))
    n_inst = (RW_WIDTH // LANES) * bsz
    return pl.pallas_call(
        functools.partial(_rwkv_body, tt=tt, bsz=bsz),
        grid=(rows // blk,),
        in_specs=[
            pl.BlockSpec((blk, RW_COLS), lambda i: (i, 0)),
            _full((1, RW_COLS)),
            pad_spec, pad_spec, pad_spec, pad_spec, pad_spec, pad_spec,
            vec_spec, vec_spec, vec_spec, vec_spec, vec_spec, vec_spec, vec_spec,
            _full((RW_WIDTH, RW_WIDTH)), _full((2 * LANES, LANES)), _full((RW_HEAD_DIM, LANES)),
        ],
        out_specs=pl.BlockSpec((blk, RW_WIDTH), lambda i: (i, 0)),
        out_shape=jax.ShapeDtypeStruct((rows, RW_WIDTH), F32),
        scratch_shapes=[
            pltpu.VMEM((bsz, RW_COLS), F32),
            pltpu.VMEM((n_inst * RW_HEAD_DIM, LANES), F32),
        ] + [pltpu.VMEM((blk, RW_WIDTH), F32)] * 6 + [pltpu.VMEM((blk + bsz, RW_WIDTH), F32)],
        compiler_params=_cparams(1),
        name="rwkv7",
    )(c, *ops)


def _rwkv_operands(mu, decay_up, decay_w0, aaa_up, aaa_a0, gate_up, k_k, k_a, r_k, ln_w, ln_b):
    def pad(w, off):
        return _wsplit(jnp.zeros((LANES, RW_WIDTH), F32).at[off:off + w.shape[0]].set(w))

    row = lambda t: t.reshape(1, RW_WIDTH)
    lane = jnp.arange(LANES)
    eye = (lane[None, :] % RW_HEAD_DIM == jnp.arange(RW_HEAD_DIM)[:, None]).astype(F32)
    return (mu.reshape(1, RW_COLS),
            *pad(decay_up, 0), *pad(aaa_up, RW_DECAY_RANK),
            *pad(gate_up, RW_DECAY_RANK + RW_AAA_RANK),
            row(decay_w0), row(aaa_a0), row(k_k), row(k_a), row(r_k), row(ln_w), row(ln_b),
            _seg_ones(RW_WIDTH, RW_HEAD_DIM),
            jnp.concatenate([_seg_ones(LANES, RW_HEAD_DIM)] * 2, axis=0), eye)


def _outproj_body(x_ref, ya_ref, yb_ref, yc_ref, wa_ref, wb_ref, wc_ref, g_ref, wq_ref,
                  keys_ref, xn_ref, h_ref, st_ref):
    xn = (x_ref[...]
          + _dot(ya_ref[...].astype(BF16), wa_ref[...])
          + _dot(yb_ref[...].astype(BF16), wb_ref[...])
          + _dot(yc_ref[...].astype(BF16), wc_ref[...]))
    xn_ref[...] = xn
    ms = jnp.mean(xn * xn, axis=-1, keepdims=True)
    h = xn * lax.rsqrt(ms + NORM_EPS) * g_ref[...]
    h_ref[...] = h
    q = _dot(h.astype(BF16), wq_ref[...]).astype(BF16)
    for hm in range(2 * PEER_HEADS):
        st_ref[hm * N_KEYS:(hm + 1) * N_KEYS, :] = lax.dot_general(
            keys_ref[hm], q[:, hm * PEER_KEY_DIM:(hm + 1) * PEER_KEY_DIM],
            (((1,), (1,)), ((), ())), preferred_element_type=F32)


def _outproj(x, ya, yb, yc, wa, wb, wc, g, wq, keys, tl):
    bsz, seq, _ = x.shape
    nl = seq // tl
    n_rows = 2 * PEER_HEADS * N_KEYS
    tok_spec = pl.BlockSpec((None, tl, D_MODEL), lambda b, i: (b, i, 0))
    tm_spec = pl.BlockSpec((tl, SSM_WIDTH), lambda b, i: (i, b))
    return pl.pallas_call(
        _outproj_body,
        grid=(bsz, nl),
        in_specs=[
            tok_spec,
            pl.BlockSpec((None, tl, DA_WIDTH), lambda b, i: (b, i, 0)),
            tm_spec, tm_spec,
            _full((DA_WIDTH, D_MODEL)), _full((SSM_WIDTH, D_MODEL)), _full((RW_WIDTH, D_MODEL)),
            _full((1, D_MODEL)),
            _full((D_MODEL, n_rows)),
            _full((2 * PEER_HEADS, N_KEYS, PEER_KEY_DIM)),
        ],
        out_specs=[
            tok_spec, tok_spec,
            pl.BlockSpec((n_rows, tl), lambda b, i: (0, b * nl + i)),
        ],
        out_shape=[
            jax.ShapeDtypeStruct((bsz, seq, D_MODEL), F32),
            jax.ShapeDtypeStruct((bsz, seq, D_MODEL), F32),
            jax.ShapeDtypeStruct((n_rows, bsz * seq), F32),
        ],
        compiler_params=_cparams(2),
        name="outproj_peer_scores",
    )(x, ya, yb, yc, wa, wb, wc, g, wq, keys)


def _topk_rows(s, k, payload=None):
    n, width = s.shape
    rows = lax.broadcasted_iota(jnp.int32, s.shape, 0).astype(F32)
    out_rows = lax.broadcasted_iota(jnp.int32, (k, width), 0)
    vals = jnp.zeros((k, width), F32)
    picks = jnp.zeros((k, width), F32)
    for j in range(k):
        m = jnp.max(s, axis=0, keepdims=True)
        key = jnp.where(s == m, rows, float(n))
        pos = jnp.min(key, axis=0, keepdims=True)
        hit = key == pos
        pick = pos if payload is None else jnp.max(jnp.where(hit, payload, -1.0), axis=0, keepdims=True)
        vals = jnp.where(out_rows == j, m, vals)
        picks = jnp.where(out_rows == j, pick, picks)
        s = jnp.where(hit, -jnp.inf, s)
    return vals, picks


def _candidates(t1, t2, combine, fill):
    k = PEER_TOPK
    sub = lax.broadcasted_iota(jnp.int32, (SUBLANES, t1.shape[1]), 0)
    blocks = [combine(t1[0:1], t2)]
    for a in range(1, SUBLANES):
        n = k // (a + 1)
        blk = combine(t1[a:a + 1], t2[0:SUBLANES])
        blocks.append(blk if n >= SUBLANES else jnp.where(sub < n, blk, fill))
    blocks.append(combine(t1[SUBLANES:k], t2[0:1]))
    return jnp.concatenate(blocks, axis=0)


def _topk_body(st_ref, idx_ref, gate_ref):
    idx_rows, gate_rows = [], []
    for h in range(PEER_HEADS):
        v1, i1 = _topk_rows(st_ref[(2 * h) * N_KEYS:(2 * h + 1) * N_KEYS, :], PEER_TOPK)
        v2, i2 = _topk_rows(st_ref[(2 * h + 1) * N_KEYS:(2 * h + 2) * N_KEYS, :], PEER_TOPK)
        cand_s = _candidates(v1, v2, lambda x, y: x + y, -jnp.inf)
        cand_e = _candidates(i1, i2, lambda x, y: x * N_KEYS + y, -1.0)
        best, expert = _topk_rows(cand_s, PEER_TOPK, payload=cand_e)
        e = jnp.exp(best - best[0:1])
        gate_rows.append(e / jnp.sum(e, axis=0, keepdims=True))
        idx_rows.append((expert * PEER_ROW_SUBLANES).astype(jnp.int32))
    gate_ref[...] = jnp.concatenate(gate_rows, axis=0)
    idx_ref[...] = jnp.concatenate(idx_rows, axis=0).T


def _topk(scores_t, tb):
    t = scores_t.shape[1]
    return pl.pallas_call(
        _topk_body,
        grid=(t // tb,),
        in_specs=[pl.BlockSpec((scores_t.shape[0], tb), lambda i: (0, i))],
        out_specs=[
            pl.BlockSpec((tb, PEER_SLOTS), lambda i: (i, 0)),
            pl.BlockSpec((PEER_SLOTS, tb), lambda i: (0, i)),
        ],
        out_shape=[
            jax.ShapeDtypeStruct((t, PEER_SLOTS), jnp.int32),
            jax.ShapeDtypeStruct((PEER_SLOTS, t), F32),
        ],
        compiler_params=_cparams(1),
        name="peer_topk",
    )(scores_t)


def _pack_body(t_ref, o_ref):
    n = t_ref.shape[0]
    bits = lax.bitcast_convert_type(t_ref[...].astype(BF16).astype(F32), jnp.uint32)
    words = (bits[:, :PEER_ROW_WORDS] >> 16) | (bits[:, PEER_ROW_WORDS:] & jnp.uint32(0xFFFF0000))
    for r in range(PEER_ROW_SUBLANES):
        o_ref[pl.ds(r, n, stride=PEER_ROW_SUBLANES), :] = words[:, r * LANES:(r + 1) * LANES]


def _pack_table(tabs, layer, te=512):
    n = tabs.shape[1]
    return pl.pallas_call(
        _pack_body,
        grid=(n // te,),
        in_specs=[pl.BlockSpec((None, te, D_MODEL), lambda i: (layer, i, 0))],
        out_specs=pl.BlockSpec((te * PEER_ROW_SUBLANES, LANES), lambda i: (i, 0)),
        out_shape=jax.ShapeDtypeStruct((n * PEER_ROW_SUBLANES, LANES), jnp.uint32),
        compiler_params=_cparams(1),
        name="peer_pack_table",
    )(tabs)


PEER_TILE_ROWS = PEER_SLOTS * PEER_ROW_SUBLANES
PEER_UNROLL = SUBLANES


def _gather_rows(tab_ref, idx_ref, tt, dst_ref, u):
    rs = PEER_ROW_SUBLANES
    for k in range(PEER_SLOTS):
        off = pl.multiple_of(idx_ref[tt, k], rs)
        dst_ref[pl.ds(u * PEER_TILE_ROWS + k * rs, rs), :] = tab_ref[pl.ds(off, rs), :]


def _row_plane(rows_ref, u, r):
    words = rows_ref[pl.ds(u * PEER_TILE_ROWS + r, PEER_SLOTS, stride=PEER_ROW_SUBLANES), :]
    lo = lax.bitcast_convert_type(words << 16, F32)
    hi = lax.bitcast_convert_type(words & jnp.uint32(0xFFFF0000), F32)
    return lo, hi


def _table_spec(tab):
    return pl.BlockSpec(tab.shape, lambda i: (0, 0), pipeline_mode=pl.Buffered(1))


def _peer_act_body(idx_ref, h_ref, gate_ref, tab_ref, coef_ref, rows_ref, act_ref, *, tb):
    lane = lax.broadcasted_iota(jnp.int32, (PEER_SLOTS, tb), 1)
    half = D_MODEL // 2
    act_ref[...] = jnp.zeros(act_ref.shape, F32)

    def group(g, carry):
        row0 = pl.multiple_of(g * PEER_UNROLL, PEER_UNROLL)
        h8 = h_ref[pl.ds(row0, PEER_UNROLL), :]
        for u in range(PEER_UNROLL):
            _gather_rows(tab_ref, idx_ref, row0 + u, rows_ref, u)
            part = jnp.zeros((PEER_SLOTS, LANES), F32)
            for r in range(PEER_ROW_SUBLANES):
                lo, hi = _row_plane(rows_ref, u, r)
                part = (part + lo * h8[u:u + 1, r * LANES:(r + 1) * LANES]
                        + hi * h8[u:u + 1, half + r * LANES:half + (r + 1) * LANES])
            act = jnp.sum(part, axis=1, keepdims=True)
            act_ref[...] = jnp.where(lane == row0 + u, act, act_ref[...])
        return carry

    lax.fori_loop(0, tb // PEER_UNROLL, group, 0)
    coef_ref[...] = _gelu(act_ref[...]) * gate_ref[...]


def _peer_act(idx, h, gate_t, tab, tb):
    t = idx.shape[0]
    slot_major = pl.BlockSpec((PEER_SLOTS, tb), lambda i: (0, i))
    return pl.pallas_call(
        functools.partial(_peer_act_body, tb=tb),
        grid=(t // tb,),
        in_specs=[
            pl.BlockSpec((tb, PEER_SLOTS), lambda i: (i, 0), memory_space=pltpu.SMEM),
            pl.BlockSpec((tb, D_MODEL), lambda i: (i, 0)),
            slot_major,
            _table_spec(tab),
        ],
        out_specs=slot_major,
        out_shape=jax.ShapeDtypeStruct((PEER_SLOTS, t), F32),
        scratch_shapes=[
            pltpu.VMEM((PEER_UNROLL * PEER_TILE_ROWS, LANES), jnp.uint32),
            pltpu.VMEM((PEER_SLOTS, tb), F32),
        ],
        compiler_params=_cparams(1),
        name="peer_act",
    )(idx, h, gate_t, tab)


def _peer_out_body(idx_ref, coef_ref, x_ref, ones_ref, tab_ref, o_ref, rows_ref, *, tb):
    lane = lax.broadcasted_iota(jnp.int32, (PEER_SLOTS, tb), 1)
    sub = lax.broadcasted_iota(jnp.int32, (PEER_UNROLL, LANES), 0)
    ones = ones_ref[...]
    n_planes = PEER_ROW_SUBLANES

    def group(g, carry):
        row0 = pl.multiple_of(g * PEER_UNROLL, PEER_UNROLL)
        planes = [jnp.zeros((PEER_UNROLL, LANES), F32) for _ in range(2 * n_planes)]
        for u in range(PEER_UNROLL):
            _gather_rows(tab_ref, idx_ref, row0 + u, rows_ref, u)
            ccol = _dot_x2k(jnp.where(lane == row0 + u, coef_ref[...], 0.0), ones)
            for r in range(n_planes):
                lo, hi = _row_plane(rows_ref, u, r)
                planes[r] = jnp.where(sub == u, jnp.sum(lo * ccol, axis=0, keepdims=True), planes[r])
                planes[n_planes + r] = jnp.where(
                    sub == u, jnp.sum(hi * ccol, axis=0, keepdims=True), planes[n_planes + r])
        o_ref[pl.ds(row0, PEER_UNROLL), :] = (x_ref[pl.ds(row0, PEER_UNROLL), :]
                                              + jnp.concatenate(planes, axis=1))
        return carry

    lax.fori_loop(0, tb // PEER_UNROLL, group, 0)


def _peer_out(idx, coef_t, x, tab, tb):
    t = idx.shape[0]
    tok_spec = pl.BlockSpec((tb, D_MODEL), lambda i: (i, 0))
    ones = jnp.ones((2 * tb, LANES), BF16)
    return pl.pallas_call(
        functools.partial(_peer_out_body, tb=tb),
        grid=(t // tb,),
        in_specs=[
            pl.BlockSpec((tb, PEER_SLOTS), lambda i: (i, 0), memory_space=pltpu.SMEM),
            pl.BlockSpec((PEER_SLOTS, tb), lambda i: (0, i)),
            tok_spec,
            _full(ones.shape),
            _table_spec(tab),
        ],
        out_specs=tok_spec,
        out_shape=jax.ShapeDtypeStruct(x.shape, F32),
        scratch_shapes=[pltpu.VMEM((PEER_UNROLL * PEER_TILE_ROWS, LANES), jnp.uint32)],
        compiler_params=_cparams(1),
        name="peer_out",
    )(idx, coef_t, x, ones, tab)


def _tiles(bsz, seq):
    tl = min(seq, 512)
    tq = min(seq, 256)
    tt_s5 = min(seq, 128)
    tt_rw = min(seq, 64)
    tb = 128
    return tl, tq, tt_s5, tt_rw, tb


def kernel(x, positions, norm_mix, w_in, da_q_norm, da_k_norm, da_lambda, da_subln, ssm_lambda_re, ssm_lambda_im, ssm_b_re, ssm_b_im, ssm_c_re, ssm_c_im, ssm_d, ssm_log_dt, ssm_w_glu, ssm_b_glu, ssm_out_norm, rw_mu, rw_decay_up, rw_decay_w0, rw_aaa_up, rw_aaa_a0, rw_gate_up, rw_k_k, rw_k_a, rw_r_k, rw_ln_w, rw_ln_b, w_out, norm_ffn, peer_w_q, peer_sub_keys, peer_u, peer_v):
    bsz, seq, _ = x.shape
    depth = w_in.shape[0]
    t = bsz * seq
    tl, tq, tt_s5, tt_rw, tb = _tiles(bsz, seq)
    rope = _rope_tables(positions, tl)

    for l in range(depth):
        w = w_in[l].astype(BF16)
        cols_a, cols_b, cols_c = _inproj(
            x, norm_mix[l].reshape(1, D_MODEL), w[:, :DA_COLS],
            w[:, DA_COLS:DA_COLS + SSM_WIDTH], w[:, DA_COLS + SSM_WIDTH:], tl)

        q, k, v = _attn_prep(cols_a.reshape(t, DA_COLS), rope,
                             da_q_norm[l].reshape(1, DA_WIDTH), da_k_norm[l].reshape(1, DA_WIDTH), tl)
        lam_init = 0.8 - 0.6 * math.exp(-0.3 * l)
        lam_pad = (jnp.zeros((SUBLANES, LANES), F32)
                   .at[0:4, 0:DA_HEAD_DIM].set(da_lambda[l]).at[4, :].set(lam_init))
        y_a = _attention(q.reshape(bsz, seq, DA_WIDTH), k.reshape(bsz, seq, DA_WIDTH),
                         v.reshape(bsz, seq, DA_WIDTH), lam_pad, da_subln[l].reshape(1, DA_V_DIM), tq)

        s5_ops = _s5_operands(ssm_lambda_re[l], ssm_lambda_im[l], ssm_b_re[l], ssm_b_im[l],
                              ssm_c_re[l], ssm_c_im[l], ssm_d[l], ssm_log_dt[l], ssm_w_glu[l],
                              ssm_b_glu[l], ssm_out_norm[l])
        y_b = _s5(cols_b.reshape(t, SSM_WIDTH), s5_ops, tt_s5, bsz)

        rw_ops = _rwkv_operands(rw_mu[l], rw_decay_up[l], rw_decay_w0[l], rw_aaa_up[l], rw_aaa_a0[l],
                                rw_gate_up[l], rw_k_k[l], rw_k_a[l], rw_r_k[l], rw_ln_w[l], rw_ln_b[l])
        y_c = _rwkv(cols_c.reshape(t, RW_COLS), rw_ops, tt_rw, bsz)

        wo = w_out[l].astype(BF16)
        keys = peer_sub_keys[l].reshape(2 * PEER_HEADS, N_KEYS, PEER_KEY_DIM).astype(BF16)
        x_mid, h_ffn, scores_t = _outproj(
            x, y_a, y_b.reshape(seq, bsz * SSM_WIDTH), y_c.reshape(seq, bsz * RW_WIDTH),
            wo[:DA_WIDTH], wo[DA_WIDTH:DA_WIDTH + SSM_WIDTH], wo[DA_WIDTH + SSM_WIDTH:],
            norm_ffn[l].reshape(1, D_MODEL), peer_w_q[l].astype(BF16), keys, tl)

        idx, gate_t = _topk(scores_t, tb)
        coef_t = _peer_act(idx, h_ffn.reshape(t, D_MODEL), gate_t, _pack_table(peer_u, l), tb)
        x = _peer_out(idx, coef_t, x_mid.reshape(t, D_MODEL),
                      _pack_table(peer_v, l), tb).reshape(bsz, seq, D_MODEL)
    return x
```

```python
import functools
import math

import jax
import jax.numpy as jnp
from jax import lax
from jax.experimental import pallas as pl
from jax.experimental.pallas import tpu as pltpu

D_MODEL = 1024
CHUNK = 64
NORM_EPS = 1e-6

DA_HEADS = 4
DA_HEAD_DIM = 64
DA_V_DIM = 128
DA_WIDTH = 512
DA_HEAD_GROUP = 4
ROT_DIM = 16
ROPE_THETA = 500000.0

SSM_GROUP = 16
SSM_GROUPS = 16
SSM_WIDTH = 256
SSM_STATE = 64

RW_HEADS = 4
RW_HEAD_DIM = 64
RW_WIDTH = 256
RW_DECAY_RANK = 32
RW_AAA_RANK = 32
RW_GATE_RANK = 64
RW_LN_EPS = 64e-5
RW_COLS = 896

DA_COLS = 1536

PEER_HEADS = 8
N_KEYS = 128
PEER_KEY_DIM = 128
PEER_TOPK = 16
PEER_SLOTS = PEER_HEADS * PEER_TOPK
PEER_ROW_WORDS = D_MODEL // 2
PEER_ROW_SUBLANES = PEER_ROW_WORDS // 128

LANES = 128
SUBLANES = 8
VMEM_LIMIT_BYTES = 56 * 1024 * 1024

F32 = jnp.float32
BF16 = jnp.bfloat16


def _cparams(n_axes, vmem=VMEM_LIMIT_BYTES):
    return pltpu.CompilerParams(
        dimension_semantics=("arbitrary",) * n_axes, vmem_limit_bytes=vmem)


def _dot(a, b):
    return jnp.dot(a, b, preferred_element_type=F32)


def _split(x):
    hi = x.astype(BF16)
    lo = (x - hi.astype(F32)).astype(BF16)
    return hi, lo


def _dot_x2(x, w):
    hi, lo = _split(x)
    return _dot(hi, w) + _dot(lo, w)


def _dot_x2k(x, w_stacked):
    hi, lo = _split(x)
    return _dot(jnp.concatenate([hi, lo], axis=1), w_stacked)


def _dot_x3(x, w_hi, w_lo):
    hi, lo = _split(x)
    return _dot(hi, w_hi) + _dot(lo, w_hi) + _dot(hi, w_lo)


def _wsplit(w):
    hi = w.astype(BF16)
    lo = (w - hi.astype(F32)).astype(BF16)
    return hi, lo


def _sigmoid(x):
    return 1.0 / (1.0 + jnp.exp(-x))


def _gelu(x):
    return 0.5 * x * (1.0 + lax.erf(x * (1.0 / math.sqrt(2.0))))


def _seg_ones(n, seg):
    r = jnp.arange(n) // seg
    return (r[:, None] == r[None, :]).astype(BF16)


def _full(shape):
    nd = len(shape)
    return pl.BlockSpec(shape, lambda *_: (0,) * nd)


def _inproj_body(x_ref, g_ref, wa_ref, wb_ref, wc_ref, a_ref, b_ref, c_ref):
    x = x_ref[...]
    ms = jnp.mean(x * x, axis=-1, keepdims=True)
    h = (x * lax.rsqrt(ms + NORM_EPS) * g_ref[...]).astype(BF16)
    a_ref[...] = _dot(h, wa_ref[...])
    b_ref[...] = _dot(h, wb_ref[...])
    c_ref[...] = _dot(h, wc_ref[...])


def _inproj(x, g, wa, wb, wc, tl):
    bsz, seq, _ = x.shape
    return pl.pallas_call(
        _inproj_body,
        grid=(bsz, seq // tl),
        in_specs=[
            pl.BlockSpec((None, tl, D_MODEL), lambda b, i: (b, i, 0)),
            _full((1, D_MODEL)),
            _full((D_MODEL, DA_COLS)),
            _full((D_MODEL, SSM_WIDTH)),
            _full((D_MODEL, RW_COLS)),
        ],
        out_specs=[
            pl.BlockSpec((None, tl, DA_COLS), lambda b, i: (b, i, 0)),
            pl.BlockSpec((tl, SSM_WIDTH), lambda b, i: (i, b)),
            pl.BlockSpec((tl, RW_COLS), lambda b, i: (i, b)),
        ],
        out_shape=[
            jax.ShapeDtypeStruct((bsz, seq, DA_COLS), F32),
            jax.ShapeDtypeStruct((seq, bsz * SSM_WIDTH), F32),
            jax.ShapeDtypeStruct((seq, bsz * RW_COLS), F32),
        ],
        compiler_params=_cparams(2),
        name="inproj",
    )(x, g, wa, wb, wc)


def _rope_body(pos_ref, invf_ref, c_ref, s1_ref, s2_ref):
    ang = pos_ref[...].astype(F32) * invf_ref[...]
    c, s = jnp.cos(ang), jnp.sin(ang)
    lm = lax.broadcasted_iota(jnp.int32, ang.shape, 1) % DA_HEAD_DIM
    half = ROT_DIM // 2
    c_ref[...] = jnp.where(lm < ROT_DIM, c, 1.0)
    s1_ref[...] = jnp.where(lm < half, -s, 0.0)
    s2_ref[...] = jnp.where((lm >= half) & (lm < ROT_DIM), s, 0.0)


def _rope_tables(positions, tm):
    t = positions.size
    half = ROT_DIM // 2
    inv_freq = ROPE_THETA ** (-jnp.arange(0, ROT_DIM, 2, dtype=F32) / ROT_DIM)
    invf = inv_freq[jnp.arange(LANES) % half][None, :]
    pos = jnp.broadcast_to(positions.reshape(t, 1), (t, LANES))
    spec = pl.BlockSpec((tm, LANES), lambda i: (i, 0))
    return pl.pallas_call(
        _rope_body,
        grid=(t // tm,),
        in_specs=[spec, _full((1, LANES))],
        out_specs=[spec, spec, spec],
        out_shape=[jax.ShapeDtypeStruct((t, LANES), F32)] * 3,
        compiler_params=_cparams(1),
        name="rope_tables",
    )(pos, invf)


def _attn_prep_body(a_ref, c_ref, s1_ref, s2_ref, qg_ref, kg_ref, seg_ref,
                    q_ref, k_ref, v_ref):
    reps = DA_WIDTH // LANES
    c = jnp.tile(c_ref[...], (1, reps))
    s1 = jnp.tile(s1_ref[...], (1, reps))
    s2 = jnp.tile(s2_ref[...], (1, reps))
    seg = seg_ref[...]
    half = ROT_DIM // 2

    def norm_rope(t, gain):
        ms = _dot_x2(t * t, seg) * (1.0 / DA_HEAD_DIM)
        tn = t * lax.rsqrt(ms + NORM_EPS) * gain
        up = pltpu.roll(tn, DA_WIDTH - half, axis=1)
        dn = pltpu.roll(tn, half, axis=1)
        return tn * c + up * s1 + dn * s2

    q = norm_rope(a_ref[:, 0:DA_WIDTH], qg_ref[...])
    k = norm_rope(a_ref[:, DA_WIDTH:2 * DA_WIDTH], kg_ref[...])
    q_ref[...] = (q * (DA_HEAD_DIM ** -0.5)).astype(BF16)
    k_ref[...] = k.astype(BF16)
    v_ref[...] = a_ref[:, 2 * DA_WIDTH:3 * DA_WIDTH].astype(BF16)


def _attn_prep(cols_a, rope, qg, kg, tm):
    t = cols_a.shape[0]
    c, s1, s2 = rope
    lane_spec = pl.BlockSpec((tm, LANES), lambda i: (i, 0))
    out_spec = pl.BlockSpec((tm, DA_WIDTH), lambda i: (i, 0))
    return pl.pallas_call(
        _attn_prep_body,
        grid=(t // tm,),
        in_specs=[
            pl.BlockSpec((tm, DA_COLS), lambda i: (i, 0)),
            lane_spec, lane_spec, lane_spec,
            _full((1, DA_WIDTH)), _full((1, DA_WIDTH)),
            _full((DA_WIDTH, DA_WIDTH)),
        ],
        out_specs=[out_spec, out_spec, out_spec],
        out_shape=[jax.ShapeDtypeStruct((t, DA_WIDTH), BF16)] * 3,
        compiler_params=_cparams(1),
        name="attn_prep",
    )(cols_a, c, s1, s2, qg, kg, _seg_ones(DA_WIDTH, DA_HEAD_DIM))


def _attn_body(q_ref, k_ref, v_ref, lam_ref, sub_ref, o_ref, m_ref, l_ref, acc_ref, s_ref, p_ref,
               *, tq, tw):
    qi = pl.program_id(2)
    lane = lax.broadcasted_iota(jnp.int32, (tq, DA_V_DIM), 1)
    q2 = []
    for h in range(DA_HEAD_GROUP):
        q = q_ref[:, h * DA_V_DIM:(h + 1) * DA_V_DIM]
        zero = jnp.zeros_like(q)
        q2.append(jnp.concatenate([jnp.where(lane < DA_HEAD_DIM, q, zero),
                                   jnp.where(lane >= DA_HEAD_DIM, q, zero)], axis=0))

    m_ref[...] = jnp.full(m_ref.shape, -jnp.inf, F32)
    l_ref[...] = jnp.zeros(l_ref.shape, F32)
    acc_ref[...] = jnp.zeros(acc_ref.shape, F32)

    def span(c0, masked):
        c0 = pl.multiple_of(c0, tw)
        if masked:
            row_chunk = (lax.broadcasted_iota(jnp.int32, (tq, tw), 0) + tq) // CHUNK
            col_chunk = (lax.broadcasted_iota(jnp.int32, (tq, tw), 1) + (c0 - qi * tq + tq)) // CHUNK
            visible = col_chunk <= row_chunk
        for h in range(DA_HEAD_GROUP):
            k = k_ref[pl.ds(c0, tw), h * DA_V_DIM:(h + 1) * DA_V_DIM]
            s2 = lax.dot_general(q2[h], k, (((1,), (1,)), ((), ())), preferred_element_type=F32)
            for m in range(2):
                s = s2[m * tq:(m + 1) * tq]
                s_ref[2 * h + m] = jnp.where(visible, s, -jnp.inf) if masked else s
        for hm in range(2 * DA_HEAD_GROUP):
            s = s_ref[hm]
            m_old = m_ref[hm]
            m_new = jnp.maximum(m_old, jnp.max(s, axis=1, keepdims=True))
            alpha = jnp.exp(m_old - m_new)
            p = jnp.exp(s - jnp.tile(m_new, (1, tw // LANES)))
            p_ref[hm] = p.astype(BF16)
            l_ref[hm] = alpha * l_ref[hm] + jnp.sum(p, axis=1, keepdims=True)
            acc_ref[hm] = alpha * acc_ref[hm]
            m_ref[hm] = m_new
        for h in range(DA_HEAD_GROUP):
            v = v_ref[pl.ds(c0, tw), h * DA_V_DIM:(h + 1) * DA_V_DIM]
            pv = _dot(jnp.concatenate([p_ref[2 * h], p_ref[2 * h + 1]], axis=0), v)
            for m in range(2):
                acc_ref[2 * h + m] += pv[m * tq:(m + 1) * tq]

    def full_span(c, carry):
        span(c * tw, False)
        return carry

    n_full = qi // (tw // tq)
    lax.fori_loop(0, n_full, full_span, 0)
    span(n_full * tw, True)

    lam = lam_ref[...]
    lam_full = (jnp.exp(jnp.sum(lam[0:1] * lam[1:2], axis=1, keepdims=True))
                - jnp.exp(jnp.sum(lam[2:3] * lam[3:4], axis=1, keepdims=True))
                + lam[4:5, 0:1])
    lam_init = lam[4:5, 0:1]
    for h in range(DA_HEAD_GROUP):
        o = acc_ref[2 * h] / l_ref[2 * h] - lam_full * (acc_ref[2 * h + 1] / l_ref[2 * h + 1])
        ms = jnp.mean(o * o, axis=-1, keepdims=True)
        o_ref[:, h * DA_V_DIM:(h + 1) * DA_V_DIM] = (
            o * lax.rsqrt(ms + NORM_EPS) * sub_ref[...] * (1.0 - lam_init))


def _attention(q, k, v, lam_pad, subln, tq):
    bsz, seq, _ = q.shape
    tw = min(2 * tq, seq)
    gw = DA_HEAD_GROUP * DA_V_DIM
    n_maps = 2 * DA_HEAD_GROUP
    kv_spec = pl.BlockSpec((None, seq, gw), lambda b, h, i: (b, 0, h))
    return pl.pallas_call(
        functools.partial(_attn_body, tq=tq, tw=tw),
        grid=(bsz, DA_HEADS // DA_HEAD_GROUP, seq // tq),
        in_specs=[
            pl.BlockSpec((None, tq, gw), lambda b, h, i: (b, i, h)),
            kv_spec, kv_spec,
            _full((SUBLANES, LANES)),
            _full((1, DA_V_DIM)),
        ],
        out_specs=pl.BlockSpec((None, tq, gw), lambda b, h, i: (b, i, h)),
        out_shape=jax.ShapeDtypeStruct((bsz, seq, DA_WIDTH), F32),
        scratch_shapes=[
            pltpu.VMEM((n_maps, tq, LANES), F32),
            pltpu.VMEM((n_maps, tq, LANES), F32),
            pltpu.VMEM((n_maps, tq, DA_V_DIM), F32),
            pltpu.VMEM((n_maps, tq, tw), F32),
            pltpu.VMEM((n_maps, tq, tw), BF16),
        ],
        compiler_params=_cparams(3),
        name="diff_attention",
    )(q, k, v, lam_pad, subln)


def _s5_body(u_ref, b_ref, are_ref, aim_ref, c_ref, d_ref, w_ref, bg_ref, g_ref, o_ref, st_ref, x_ref,
             *, tt, bsz):
    ns = SSM_GROUPS * SSM_STATE

    @pl.when(pl.program_id(0) == 0)
    def _():
        st_ref[...] = jnp.zeros(st_ref.shape, F32)

    u = u_ref[...]
    x_ref[...] = _dot(u.astype(BF16), b_ref[...])
    a_re = jnp.broadcast_to(are_ref[...], (bsz, ns))
    a_im = jnp.broadcast_to(aim_ref[...], (bsz, ns))

    def step(t, carry):
        xr, xi = carry
        r = pl.multiple_of(t * bsz, bsz)
        nr = a_re * xr - a_im * xi + x_ref[pl.ds(r, bsz), 0:ns]
        ni = a_re * xi + a_im * xr + x_ref[pl.ds(r, bsz), ns:2 * ns]
        x_ref[pl.ds(r, bsz), 0:ns] = nr
        x_ref[pl.ds(r, bsz), ns:2 * ns] = ni
        return nr, ni

    xr, xi = lax.fori_loop(0, tt, step, (st_ref[:, 0:ns], st_ref[:, ns:2 * ns]))
    st_ref[:, 0:ns] = xr
    st_ref[:, ns:2 * ns] = xi

    y = _dot(x_ref[...].astype(BF16), c_ref[...]) + d_ref[...] * u
    yg = _gelu(y)
    z = _dot(yg.astype(BF16), w_ref[...]) + bg_ref[...]
    out = yg * _sigmoid(z)
    ms = jnp.mean(out * out, axis=-1, keepdims=True)
    o_ref[...] = out * lax.rsqrt(ms + NORM_EPS) * g_ref[...]


def _s5(u, ops, tt, bsz):
    rows = u.shape[0]
    ns2 = 2 * SSM_GROUPS * SSM_STATE
    blk = tt * bsz
    io_spec = pl.BlockSpec((blk, SSM_WIDTH), lambda i: (i, 0))
    return pl.pallas_call(
        functools.partial(_s5_body, tt=tt, bsz=bsz),
        grid=(rows // blk,),
        in_specs=[
            io_spec,
            _full((SSM_WIDTH, ns2)),
            _full((1, ns2 // 2)), _full((1, ns2 // 2)),
            _full((ns2, SSM_WIDTH)),
            _full((1, SSM_WIDTH)),
            _full((SSM_WIDTH, SSM_WIDTH)),
            _full((1, SSM_WIDTH)), _full((1, SSM_WIDTH)),
        ],
        out_specs=io_spec,
        out_shape=jax.ShapeDtypeStruct((rows, SSM_WIDTH), F32),
        scratch_shapes=[
            pltpu.VMEM((bsz, ns2), F32),
            pltpu.VMEM((blk, ns2), F32),
        ],
        compiler_params=_cparams(1),
        name="s5",
    )(u, *ops)


def _s5_operands(lam_re, lam_im, b_re, b_im, c_re, c_im, d_skip, log_dt, w_glu, b_glu, out_norm):
    lr = jnp.minimum(lam_re, -1e-4)
    li = lam_im
    step = jnp.exp(log_dt)[:, None]
    mag = jnp.exp(lr * step)
    abar_re = mag * jnp.cos(li * step)
    abar_im = mag * jnp.sin(li * step)
    den = lr * lr + li * li
    coef_re = ((abar_re - 1.0) * lr + abar_im * li) / den
    coef_im = (abar_im * lr - (abar_re - 1.0) * li) / den
    bbar_re = coef_re[..., None] * b_re - coef_im[..., None] * b_im
    bbar_im = coef_re[..., None] * b_im + coef_im[..., None] * b_re
    eye = jnp.eye(SSM_GROUPS, dtype=F32)
    ns = SSM_GROUPS * SSM_STATE

    def in_mat(bb):
        return jnp.einsum('gph,gk->ghkp', bb, eye).reshape(SSM_WIDTH, ns)

    def out_mat(cc):
        return jnp.einsum('ghp,gk->kpgh', cc, eye).reshape(ns, SSM_WIDTH)

    bmat = jnp.concatenate([in_mat(bbar_re), in_mat(bbar_im)], axis=1)
    cmat = jnp.concatenate([out_mat(c_re), -out_mat(c_im)], axis=0)
    return (bmat.astype(BF16), abar_re.reshape(1, ns), abar_im.reshape(1, ns), cmat.astype(BF16),
            d_skip.reshape(1, SSM_WIDTH), w_glu.astype(BF16), b_glu.reshape(1, SSM_WIDTH),
            out_norm.reshape(1, SSM_WIDTH))


def _rwkv_body(c_ref, mu_ref, wdh_ref, wdl_ref, wah_ref, wal_ref, wgh_ref, wgl_ref,
               w0_ref, a0_ref, kk_ref, ka_ref, rk_ref, lnw_ref, lnb_ref,
               seg256_ref, seg128_ref, eye_ref,
               o_ref, prev_ref, st_ref, r_s, w_s, k_s, v_s, a_s, b_s, y_s, *, tt, bsz):
    n_pairs = RW_WIDTH // LANES
    n_inst = n_pairs * bsz
    hd = RW_HEAD_DIM

    @pl.when(pl.program_id(0) == 0)
    def _():
        prev_ref[...] = jnp.zeros(prev_ref.shape, F32)
        st_ref[...] = jnp.zeros(st_ref.shape, F32)

    c = c_ref[...]
    rows = c.shape[0]
    prev = jnp.concatenate([prev_ref[...], c[0:rows - bsz]], axis=0)
    prev_ref[...] = c[rows - bsz:rows]
    cs = c + (prev - c) * mu_ref[...]
    r = cs[:, 0:RW_WIDTH]
    k = cs[:, RW_WIDTH:2 * RW_WIDTH]
    v = cs[:, 2 * RW_WIDTH:3 * RW_WIDTH]
    x3 = cs[:, 3 * RW_WIDTH:RW_COLS]

    seg256 = seg256_ref[...]
    wdec = w0_ref[...] + _dot_x3(jnp.tanh(x3), wdh_ref[...], wdl_ref[...])
    z = -wdec
    softplus = jnp.maximum(z, 0.0) + jnp.log(1.0 + jnp.exp(-jnp.abs(z)))
    decay = jnp.exp(-jnp.exp(-softplus - 0.5))
    a = _sigmoid(a0_ref[...] + _dot_x3(x3, wah_ref[...], wal_ref[...]))
    g = _dot_x3(_sigmoid(x3), wgh_ref[...], wgl_ref[...])
    kk = k * kk_ref[...]
    kk = kk / jnp.maximum(jnp.sqrt(_dot_x2(kk * kk, seg256)), 1e-12)
    k2 = k * (1.0 + (a - 1.0) * ka_ref[...])
    r_s[...] = r
    w_s[...] = decay
    k_s[...] = k2
    v_s[...] = v
    a_s[...] = -kk
    b_s[...] = kk * a

    seg128x2 = seg128_ref[...]
    seg128 = seg128x2[0:LANES]
    eye = eye_ref[...]
    sub = lax.broadcasted_iota(jnp.int32, (bsz, LANES), 0)

    def vec_of(block, inst):
        p, b = divmod(inst, bsz)
        return block[b:b + 1, p * LANES:(p + 1) * LANES]

    def read_out(states, r_blk, dst_row):
        lhs_y = jnp.concatenate([states[i] * vec_of(r_blk, i) for i in range(n_inst)], axis=0)
        ycol = _dot(lhs_y.astype(BF16), seg128)
        for p in range(n_pairs):
            yblk = jnp.zeros((bsz, LANES), F32)
            for b in range(bsz):
                i = p * bsz + b
                yrow = jnp.sum(eye * ycol[i * hd:(i + 1) * hd], axis=0, keepdims=True)
                yblk = jnp.where(sub == b, yrow, yblk)
            y_s[pl.ds(dst_row, bsz), p * LANES:(p + 1) * LANES] = yblk

    def step(t, carry):
        row0 = pl.multiple_of(t * bsz, bsz)
        blk = [ref[pl.ds(row0, bsz), :] for ref in (w_s, k_s, v_s, a_s, b_s)]
        vec = lambda q, inst: vec_of(blk[q], inst)
        states = [st_ref[i * hd:(i + 1) * hd, :] for i in range(n_inst)]
        prev_row = pl.multiple_of(jnp.maximum(row0 - bsz, 0), bsz)
        read_out(states, r_s[pl.ds(prev_row, bsz), :], row0)
        halves = [range(g, g + n_inst // 2) for g in (0, n_inst // 2)]
        lhs_a = [jnp.concatenate([states[i] * vec(3, i) for i in h], axis=0) for h in halves]
        lhs_v = [jnp.concatenate([eye * vec(2, i) for i in h], axis=0) for h in halves]
        sa = [_dot(x.astype(BF16), seg128) for x in lhs_a]
        vcol = [_dot(x.astype(BF16), seg128) for x in lhs_v]
        for g, h in enumerate(halves):
            for n, i in enumerate(h):
                sl = slice(n * hd, (n + 1) * hd)
                st_ref[i * hd:(i + 1) * hd, :] = (states[i] * vec(0, i) + sa[g][sl] * vec(4, i)
                                                  + vcol[g][sl] * vec(1, i))
        return carry

    lax.fori_loop(0, tt, step, 0)
    last = (tt - 1) * bsz
    read_out([st_ref[i * hd:(i + 1) * hd, :] for i in range(n_inst)], r_s[last:last + bsz, :], tt * bsz)

    y = y_s[bsz:(tt + 1) * bsz, :]
    inv = 1.0 / hd
    mean = _dot_x2(y, seg256) * inv
    dv = y - mean
    var = _dot_x2(dv * dv, seg256) * inv
    yn = dv * lax.rsqrt(var + RW_LN_EPS) * lnw_ref[...] + lnb_ref[...]
    bonus = _dot_x2(r * k2 * rk_ref[...], seg256) * v
    o_ref[...] = (yn + bonus) * g


def _rwkv(c, ops, tt, bsz):
    rows = c.shape[0]
    blk = tt * bsz
    vec_spec = _full((1, RW_WIDTH))
    pad_spec = _full((LANES, RW_WIDTH))
    n_inst = (RW_WIDTH // LANES) * bsz
    return pl.pallas_call(
        functools.partial(_rwkv_body, tt=tt, bsz=bsz),
        grid=(rows // blk,),
        in_specs=[
            pl.BlockSpec((blk, RW_COLS), lambda i: (i, 0)),
            _full((1, RW_COLS)),
            pad_spec, pad_spec, pad_spec, pad_spec, pad_spec, pad_spec,
            vec_spec, vec_spec, vec_spec, vec_spec, vec_spec, vec_spec, vec_spec,
            _full((RW_WIDTH, RW_WIDTH)), _full((2 * LANES, LANES)), _full((RW_HEAD_DIM, LANES)),
        ],
        out_specs=pl.BlockSpec((blk, RW_WIDTH), lambda i: (i, 0)),
        out_shape=jax.ShapeDtypeStruct((rows, RW_WIDTH), F32),
        scratch_shapes=[
            pltpu.VMEM((bsz, RW_COLS), F32),
            pltpu.VMEM((n_inst * RW_HEAD_DIM, LANES), F32),
        ] + [pltpu.VMEM((blk, RW_WIDTH), F32)] * 6 + [pltpu.VMEM((blk + bsz, RW_WIDTH), F32)],
        compiler_params=_cparams(1),
        name="rwkv7",
    )(c, *ops)


def _rwkv_operands(mu, decay_up, decay_w0, aaa_up, aaa_a0, gate_up, k_k, k_a, r_k, ln_w, ln_b):
    def pad(w, off):
        return _wsplit(jnp.zeros((LANES, RW_WIDTH), F32).at[off:off + w.shape[0]].set(w))

    row = lambda t: t.reshape(1, RW_WIDTH)
    lane = jnp.arange(LANES)
    eye = (lane[None, :] % RW_HEAD_DIM == jnp.arange(RW_HEAD_DIM)[:, None]).astype(F32)
    return (mu.reshape(1, RW_COLS),
            *pad(decay_up, 0), *pad(aaa_up, RW_DECAY_RANK),
            *pad(gate_up, RW_DECAY_RANK + RW_AAA_RANK),
            row(decay_w0), row(aaa_a0), row(k_k), row(k_a), row(r_k), row(ln_w), row(ln_b),
            _seg_ones(RW_WIDTH, RW_HEAD_DIM),
            jnp.concatenate([_seg_ones(LANES, RW_HEAD_DIM)] * 2, axis=0), eye)


def _outproj_body(x_ref, ya_ref, yb_ref, yc_ref, wa_ref, wb_ref, wc_ref, g_ref, wq_ref,
                  keys_ref, xn_ref, h_ref, st_ref):
    xn = (x_ref[...]
          + _dot(ya_ref[...].astype(BF16), wa_ref[...])
          + _dot(yb_ref[...].astype(BF16), wb_ref[...])
          + _dot(yc_ref[...].astype(BF16), wc_ref[...]))
    xn_ref[...] = xn
    ms = jnp.mean(xn * xn, axis=-1, keepdims=True)
    h = xn * lax.rsqrt(ms + NORM_EPS) * g_ref[...]
    h_ref[...] = h
    q = _dot(h.astype(BF16), wq_ref[...]).astype(BF16)
    for hm in range(2 * PEER_HEADS):
        st_ref[hm * N_KEYS:(hm + 1) * N_KEYS, :] = lax.dot_general(
            keys_ref[hm], q[:, hm * PEER_KEY_DIM:(hm + 1) * PEER_KEY_DIM],
            (((1,), (1,)), ((), ())), preferred_element_type=F32)


def _outproj(x, ya, yb, yc, wa, wb, wc, g, wq, keys, tl):
    bsz, seq, _ = x.shape
    nl = seq // tl
    n_rows = 2 * PEER_HEADS * N_KEYS
    tok_spec = pl.BlockSpec((None, tl, D_MODEL), lambda b, i: (b, i, 0))
    tm_spec = pl.BlockSpec((tl, SSM_WIDTH), lambda b, i: (i, b))
    return pl.pallas_call(
        _outproj_body,
        grid=(bsz, nl),
        in_specs=[
            tok_spec,
            pl.BlockSpec((None, tl, DA_WIDTH), lambda b, i: (b, i, 0)),
            tm_spec, tm_spec,
            _full((DA_WIDTH, D_MODEL)), _full((SSM_WIDTH, D_MODEL)), _full((RW_WIDTH, D_MODEL)),
            _full((1, D_MODEL)),
            _full((D_MODEL, n_rows)),
            _full((2 * PEER_HEADS, N_KEYS, PEER_KEY_DIM)),
        ],
        out_specs=[
            tok_spec, tok_spec,
            pl.BlockSpec((n_rows, tl), lambda b, i: (0, b * nl + i)),
        ],
        out_shape=[
            jax.ShapeDtypeStruct((bsz, seq, D_MODEL), F32),
            jax.ShapeDtypeStruct((bsz, seq, D_MODEL), F32),
            jax.ShapeDtypeStruct((n_rows, bsz * seq), F32),
        ],
        compiler_params=_cparams(2),
        name="outproj_peer_scores",
    )(x, ya, yb, yc, wa, wb, wc, g, wq, keys)


def _topk_rows(s, k, payload=None):
    n, width = s.shape
    rows = lax.broadcasted_iota(jnp.int32, s.shape, 0).astype(F32)
    out_rows = lax.broadcasted_iota(jnp.int32, (k, width), 0)
    vals = jnp.zeros((k, width), F32)
    picks = jnp.zeros((k, width), F32)
    for j in range(k):
        m = jnp.max(s, axis=0, keepdims=True)
        key = jnp.where(s == m, rows, float(n))
        pos = jnp.min(key, axis=0, keepdims=True)
        hit = key == pos
        pick = pos if payload is None else jnp.max(jnp.where(hit, payload, -1.0), axis=0, keepdims=True)
        vals = jnp.where(out_rows == j, m, vals)
        picks = jnp.where(out_rows == j, pick, picks)
        s = jnp.where(hit, -jnp.inf, s)
    return vals, picks


def _candidates(t1, t2, combine, fill):
    k = PEER_TOPK
    sub = lax.broadcasted_iota(jnp.int32, (SUBLANES, t1.shape[1]), 0)
    blocks = [combine(t1[0:1], t2)]
    for a in range(1, SUBLANES):
        n = k // (a + 1)
        blk = combine(t1[a:a + 1], t2[0:SUBLANES])
        blocks.append(blk if n >= SUBLANES else jnp.where(sub < n, blk, fill))
    blocks.append(combine(t1[SUBLANES:k], t2[0:1]))
    return jnp.concatenate(blocks, axis=0)


def _topk_body(st_ref, idx_ref, gate_ref):
    idx_rows, gate_rows = [], []
    for h in range(PEER_HEADS):
        v1, i1 = _topk_rows(st_ref[(2 * h) * N_KEYS:(2 * h + 1) * N_KEYS, :], PEER_TOPK)
        v2, i2 = _topk_rows(st_ref[(2 * h + 1) * N_KEYS:(2 * h + 2) * N_KEYS, :], PEER_TOPK)
        cand_s = _candidates(v1, v2, lambda x, y: x + y, -jnp.inf)
        cand_e = _candidates(i1, i2, lambda x, y: x * N_KEYS + y, -1.0)
        best, expert = _topk_rows(cand_s, PEER_TOPK, payload=cand_e)
        e = jnp.exp(best - best[0:1])
        gate_rows.append(e / jnp.sum(e, axis=0, keepdims=True))
        idx_rows.append((expert * PEER_ROW_SUBLANES).astype(jnp.int32))
    gate_ref[...] = jnp.concatenate(gate_rows, axis=0)
    idx_ref[...] = jnp.concatenate(idx_rows, axis=0).T


def _topk(scores_t, tb):
    t = scores_t.shape[1]
    return pl.pallas_call(
        _topk_body,
        grid=(t // tb,),
        in_specs=[pl.BlockSpec((scores_t.shape[0], tb), lambda i: (0, i))],
        out_specs=[
            pl.BlockSpec((tb, PEER_SLOTS), lambda i: (i, 0)),
            pl.BlockSpec((PEER_SLOTS, tb), lambda i: (0, i)),
        ],
        out_shape=[
            jax.ShapeDtypeStruct((t, PEER_SLOTS), jnp.int32),
            jax.ShapeDtypeStruct((PEER_SLOTS, t), F32),
        ],
        compiler_params=_cparams(1),
        name="peer_topk",
    )(scores_t)


def _pack_body(t_ref, o_ref):
    n = t_ref.shape[0]
    bits = lax.bitcast_convert_type(t_ref[...].astype(BF16).astype(F32), jnp.uint32)
    words = (bits[:, :PEER_ROW_WORDS] >> 16) | (bits[:, PEER_ROW_WORDS:] & jnp.uint32(0xFFFF0000))
    for r in range(PEER_ROW_SUBLANES):
        o_ref[pl.ds(r, n, stride=PEER_ROW_SUBLANES), :] = words[:, r * LANES:(r + 1) * LANES]


def _pack_table(tabs, layer, te=512):
    n = tabs.shape[1]
    return pl.pallas_call(
        _pack_body,
        grid=(n // te,),
        in_specs=[pl.BlockSpec((None, te, D_MODEL), lambda i: (layer, i, 0))],
        out_specs=pl.BlockSpec((te * PEER_ROW_SUBLANES, LANES), lambda i: (i, 0)),
        out_shape=jax.ShapeDtypeStruct((n * PEER_ROW_SUBLANES, LANES), jnp.uint32),
        compiler_params=_cparams(1),
        name="peer_pack_table",
    )(tabs)


PEER_TILE_ROWS = PEER_SLOTS * PEER_ROW_SUBLANES
PEER_UNROLL = 4 * SUBLANES


def _gather_rows(tab_ref, idx_ref, tt, dst_ref, u):
    rs = PEER_ROW_SUBLANES
    for k in range(PEER_SLOTS):
        off = pl.multiple_of(idx_ref[tt, k], rs)
        dst_ref[pl.ds(u * PEER_TILE_ROWS + k * rs, rs), :] = tab_ref[pl.ds(off, rs), :]


def _row_plane(rows_ref, u, r):
    words = rows_ref[pl.ds(u * PEER_TILE_ROWS + r, PEER_SLOTS, stride=PEER_ROW_SUBLANES), :]
    lo = lax.bitcast_convert_type(words << 16, F32)
    hi = lax.bitcast_convert_type(words & jnp.uint32(0xFFFF0000), F32)
    return lo, hi


def _table_spec(tab):
    return pl.BlockSpec(tab.shape, lambda i: (0, 0), pipeline_mode=pl.Buffered(1))


def _peer_act_body(idx_ref, h_ref, gate_ref, tab_ref, coef_ref, rows_ref, act_ref, *, tb):
    lane = lax.broadcasted_iota(jnp.int32, (PEER_SLOTS, tb), 1)
    half = D_MODEL // 2
    act_ref[...] = jnp.zeros(act_ref.shape, F32)

    def group(g, carry):
        row0 = pl.multiple_of(g * PEER_UNROLL, PEER_UNROLL)
        h8 = h_ref[pl.ds(row0, PEER_UNROLL), :]
        for u in range(PEER_UNROLL):
            _gather_rows(tab_ref, idx_ref, row0 + u, rows_ref, u)
            part = jnp.zeros((PEER_SLOTS, LANES), F32)
            for r in range(PEER_ROW_SUBLANES):
                lo, hi = _row_plane(rows_ref, u, r)
                part = (part + lo * h8[u:u + 1, r * LANES:(r + 1) * LANES]
                        + hi * h8[u:u + 1, half + r * LANES:half + (r + 1) * LANES])
            act = jnp.sum(part, axis=1, keepdims=True)
            act_ref[...] = jnp.where(lane == row0 + u, act, act_ref[...])
        return carry

    lax.fori_loop(0, tb // PEER_UNROLL, group, 0)
    coef_ref[...] = _gelu(act_ref[...]) * gate_ref[...]


def _peer_act(idx, h, gate_t, tab, tb):
    t = idx.shape[0]
    slot_major = pl.BlockSpec((PEER_SLOTS, tb), lambda i: (0, i))
    return pl.pallas_call(
        functools.partial(_peer_act_body, tb=tb),
        grid=(t // tb,),
        in_specs=[
            pl.BlockSpec((tb, PEER_SLOTS), lambda i: (i, 0), memory_space=pltpu.SMEM),
            pl.BlockSpec((tb, D_MODEL), lambda i: (i, 0)),
            slot_major,
            _table_spec(tab),
        ],
        out_specs=slot_major,
        out_shape=jax.ShapeDtypeStruct((PEER_SLOTS, t), F32),
        scratch_shapes=[
            pltpu.VMEM((PEER_UNROLL * PEER_TILE_ROWS, LANES), jnp.uint32),
            pltpu.VMEM((PEER_SLOTS, tb), F32),
        ],
        compiler_params=_cparams(1),
        name="peer_act",
    )(idx, h, gate_t, tab)


def _peer_out_body(idx_ref, coef_ref, x_ref, ones_ref, tab_ref, o_ref, rows_ref, *, tb):
    lane = lax.broadcasted_iota(jnp.int32, (PEER_SLOTS, tb), 1)
    sub = lax.broadcasted_iota(jnp.int32, (PEER_UNROLL, LANES), 0)
    ones = ones_ref[...]
    n_planes = PEER_ROW_SUBLANES

    def group(g, carry):
        row0 = pl.multiple_of(g * PEER_UNROLL, PEER_UNROLL)
        planes = [jnp.zeros((PEER_UNROLL, LANES), F32) for _ in range(2 * n_planes)]
        for u in range(PEER_UNROLL):
            _gather_rows(tab_ref, idx_ref, row0 + u, rows_ref, u)
            ccol = _dot_x2k(jnp.where(lane == row0 + u, coef_ref[...], 0.0), ones)
            for r in range(n_planes):
                lo, hi = _row_plane(rows_ref, u, r)
                planes[r] = jnp.where(sub == u, jnp.sum(lo * ccol, axis=0, keepdims=True), planes[r])
                planes[n_planes + r] = jnp.where(
                    sub == u, jnp.sum(hi * ccol, axis=0, keepdims=True), planes[n_planes + r])
        o_ref[pl.ds(row0, PEER_UNROLL), :] = (x_ref[pl.ds(row0, PEER_UNROLL), :]
                                              + jnp.concatenate(planes, axis=1))
        return carry

    lax.fori_loop(0, tb // PEER_UNROLL, group, 0)


def _peer_out(idx, coef_t, x, tab, tb):
    t = idx.shape[0]
    tok_spec = pl.BlockSpec((tb, D_MODEL), lambda i: (i, 0))
    ones = jnp.ones((2 * tb, LANES), BF16)
    return pl.pallas_call(
        functools.partial(_peer_out_body, tb=tb),
        grid=(t // tb,),
        in_specs=[
            pl.BlockSpec((tb, PEER_SLOTS), lambda i: (i, 0), memory_space=pltpu.SMEM),
            pl.BlockSpec((PEER_SLOTS, tb), lambda i: (0, i)),
            tok_spec,
            _full(ones.shape),
            _table_spec(tab),
        ],
        out_specs=tok_spec,
        out_shape=jax.ShapeDtypeStruct(x.shape, F32),
        scratch_shapes=[pltpu.VMEM((PEER_UNROLL * PEER_TILE_ROWS, LANES), jnp.uint32)],
        compiler_params=_cparams(1),
        name="peer_out",
    )(idx, coef_t, x, ones, tab)


def _tiles(bsz, seq):
    tl = min(seq, 512)
    tq = min(seq, 256)
    tt_s5 = min(seq, 128)
    tt_rw = min(seq, 64)
    tb = 128
    return tl, tq, tt_s5, tt_rw, tb


def kernel(x, positions, norm_mix, w_in, da_q_norm, da_k_norm, da_lambda, da_subln, ssm_lambda_re, ssm_lambda_im, ssm_b_re, ssm_b_im, ssm_c_re, ssm_c_im, ssm_d, ssm_log_dt, ssm_w_glu, ssm_b_glu, ssm_out_norm, rw_mu, rw_decay_up, rw_decay_w0, rw_aaa_up, rw_aaa_a0, rw_gate_up, rw_k_k, rw_k_a, rw_r_k, rw_ln_w, rw_ln_b, w_out, norm_ffn, peer_w_q, peer_sub_keys, peer_u, peer_v):
    bsz, seq, _ = x.shape
    depth = w_in.shape[0]
    t = bsz * seq
    tl, tq, tt_s5, tt_rw, tb = _tiles(bsz, seq)
    rope = _rope_tables(positions, tl)

    for l in range(depth):
        w = w_in[l].astype(BF16)
        cols_a, cols_b, cols_c = _inproj(
            x, norm_mix[l].reshape(1, D_MODEL), w[:, :DA_COLS],
            w[:, DA_COLS:DA_COLS + SSM_WIDTH], w[:, DA_COLS + SSM_WIDTH:], tl)

        q, k, v = _attn_prep(cols_a.reshape(t, DA_COLS), rope,
                             da_q_norm[l].reshape(1, DA_WIDTH), da_k_norm[l].reshape(1, DA_WIDTH), tl)
        lam_init = 0.8 - 0.6 * math.exp(-0.3 * l)
        lam_pad = (jnp.zeros((SUBLANES, LANES), F32)
                   .at[0:4, 0:DA_HEAD_DIM].set(da_lambda[l]).at[4, :].set(lam_init))
        y_a = _attention(q.reshape(bsz, seq, DA_WIDTH), k.reshape(bsz, seq, DA_WIDTH),
                         v.reshape(bsz, seq, DA_WIDTH), lam_pad, da_subln[l].reshape(1, DA_V_DIM), tq)

        s5_ops = _s5_operands(ssm_lambda_re[l], ssm_lambda_im[l], ssm_b_re[l], ssm_b_im[l],
                              ssm_c_re[l], ssm_c_im[l], ssm_d[l], ssm_log_dt[l], ssm_w_glu[l],
                              ssm_b_glu[l], ssm_out_norm[l])
        y_b = _s5(cols_b.reshape(t, SSM_WIDTH), s5_ops, tt_s5, bsz)

        rw_ops = _rwkv_operands(rw_mu[l], rw_decay_up[l], rw_decay_w0[l], rw_aaa_up[l], rw_aaa_a0[l],
                                rw_gate_up[l], rw_k_k[l], rw_k_a[l], rw_r_k[l], rw_ln_w[l], rw_ln_b[l])
        y_c = _rwkv(cols_c.reshape(t, RW_COLS), rw_ops, tt_rw, bsz)

        wo = w_out[l].astype(BF16)
        keys = peer_sub_keys[l].reshape(2 * PEER_HEADS, N_KEYS, PEER_KEY_DIM).astype(BF16)
        x_mid, h_ffn, scores_t = _outproj(
            x, y_a, y_b.reshape(seq, bsz * SSM_WIDTH), y_c.reshape(seq, bsz * RW_WIDTH),
            wo[:DA_WIDTH], wo[DA_WIDTH:DA_WIDTH + SSM_WIDTH], wo[DA_WIDTH + SSM_WIDTH:],
            norm_ffn[l].reshape(1, D_MODEL), peer_w_q[l].astype(BF16), keys, tl)

        idx, gate_t = _topk(scores_t, tb)
        coef_t = _peer_act(idx, h_ffn.reshape(t, D_MODEL), gate_t, _pack_table(peer_u, l), tb)
        x = _peer_out(idx, coef_t, x_mid.reshape(t, D_MODEL),
                      _pack_table(peer_v, l), tb).reshape(bsz, seq, D_MODEL)
    return x
```
